```python
import jax
import jax.numpy as jnp
from jax import lax
import numpy as np

D_MODEL = 1024
BATCH = 16
SEQ = 2048
DEPTH = 1

N_MEM = 256
BLOCK = 128
FOX_HEADS = 8
FOX_HD = 64
SWA_HEADS = 8
SWA_KV_HEADS = 2
SWA_HD = 64
WINDOW = 128
MEM_HEADS = 4
MEM_HD = 128
N_BRANCH = 3
PEER_HEADS = 8
N_KEYS = 128
N_EXPERTS = N_KEYS * N_KEYS
PEER_QD = 256
PEER_TOPK = 16
PEER_CHUNK = 128
EPS = 1e-6
NEG_INF = -1e30

FOX_W = FOX_HEADS * FOX_HD
SWA_W = SWA_HEADS * SWA_HD
SWA_KV_W = SWA_KV_HEADS * SWA_HD
MEM_W = MEM_HEADS * MEM_HD
IN_WIDTHS = (FOX_W, FOX_W, FOX_W, FOX_HEADS, SWA_W, SWA_KV_W, SWA_KV_W, MEM_W, N_BRANCH * D_MODEL)
IN_COLS = 3 * FOX_W + FOX_HEADS + SWA_W + 2 * SWA_KV_W + MEM_W + N_BRANCH * D_MODEL

kernel_name = 'hybrid_fox_swa_mem_peer_block'


def rmsnorm(x, g):
    xf = x.astype(jnp.float32)
    y = xf * lax.rsqrt(jnp.mean(xf * xf, axis=-1, keepdims=True) + EPS)
    return (y * g.astype(jnp.float32)).astype(x.dtype)


def head_rmsnorm(x, g):
    xf = x.astype(jnp.float32)
    return xf * lax.rsqrt(jnp.mean(xf * xf, axis=-1, keepdims=True) + EPS) * g.astype(jnp.float32)


def alibi_slopes(n):
    return 2.0 ** (-8.0 * (jnp.arange(n, dtype=jnp.float32) + 1.0) / n)


def fox_attention(q, k, v, log_f):
    B, S, H, d = q.shape
    nb = S // BLOCK
    c = jnp.cumsum(log_f, axis=1).transpose(0, 2, 1)
    q_blocks = q.reshape(B, nb, BLOCK, H, d).transpose(1, 0, 2, 3, 4)
    c_blocks = c.reshape(B, H, nb, BLOCK).transpose(2, 0, 1, 3)
    vf = v.astype(jnp.float32)
    kpos = jnp.arange(S)
    scale = d ** -0.5

    def one_block(args):
        i, q_i, c_i = args
        s = jnp.einsum('bqhd,bkhd->bhqk', q_i, k) * scale
        s = s + (c_i[..., :, None] - c[..., None, :])
        qpos = i * BLOCK + jnp.arange(BLOCK)
        s = jnp.where(kpos[None, :] <= qpos[:, None], s, NEG_INF)
        p = jax.nn.softmax(s, axis=-1)
        return jnp.einsum('bhqk,bkhd->bqhd', p, vf)

    o = lax.map(one_block, (jnp.arange(nb), q_blocks, c_blocks))
    return o.transpose(1, 0, 2, 3, 4).reshape(B, S, H, d)


def swa_attention(q, k, v, sinks, slopes):
    B, S, Hq, d = q.shape
    G = k.shape[2]
    R = Hq // G
    nb = S // BLOCK
    qb = q.reshape(B, nb, BLOCK, G, R, d)

    def band(t):
        t = t.astype(jnp.float32)
        tp = jnp.concatenate([jnp.zeros((B, BLOCK, G, d), jnp.float32), t], axis=1)
        tp = tp.reshape(B, nb + 1, BLOCK, G, d)
        return jnp.concatenate([tp[:, :-1], tp[:, 1:]], axis=2)

    kb = band(k)
    vb = band(v)
    s = jnp.einsum('bnqgrd,bnkgd->bngrqk', qb, kb) * d ** -0.5
    qi = jnp.arange(BLOCK)[:, None]
    kj = jnp.arange(2 * BLOCK)[None, :]
    dist_i = qi + BLOCK - kj
    in_window = (dist_i >= 0) & (dist_i < WINDOW)
    blk_ok = (jnp.arange(nb)[:, None] > 0) | (jnp.arange(2 * BLOCK)[None, :] >= BLOCK)
    mask = in_window[None] & blk_ok[:, None, :]
    s = s - slopes.astype(jnp.float32).reshape(G, R)[:, :, None, None] * dist_i.astype(jnp.float32)
    s = jnp.where(mask[None, :, None, None], s, NEG_INF)
    sink = jnp.broadcast_to(sinks.astype(jnp.float32).reshape(1, 1, G, R, 1, 1), s.shape[:-1] + (1,))
    p = jax.nn.softmax(jnp.concatenate([s, sink], axis=-1), axis=-1)[..., :-1]
    o = jnp.einsum('bngrqk,bnkgd->bnqgrd', p, vb)
    return o.reshape(B, S, Hq, d)


def memory_attention(q, k, v):
    d = q.shape[-1]
    s = jnp.einsum('bshd,bmhd->bhsm', q, k) * d ** -0.5
    p = jax.nn.softmax(s, axis=-1)
    return jnp.einsum('bhsm,bmhd->bshd', p, v.astype(jnp.float32))


def peer_ffn(h, w_q, keys1, keys2, u, v):
    B, S, D = h.shape
    half = PEER_QD // 2
    q = (h @ w_q).astype(jnp.float32).reshape(B, S, PEER_HEADS, 2, half)
    s1 = jnp.einsum('bshd,hkd->bshk', q[..., 0, :], keys1.astype(jnp.float32))
    s2 = jnp.einsum('bshd,hkd->bshk', q[..., 1, :], keys2.astype(jnp.float32))
    v1, i1 = lax.top_k(s1, PEER_TOPK)
    v2, i2 = lax.top_k(s2, PEER_TOPK)
    cand = (v1[..., :, None] + v2[..., None, :]).reshape(B, S, PEER_HEADS, PEER_TOPK * PEER_TOPK)
    sv, si = lax.top_k(cand, PEER_TOPK)
    e1 = jnp.take_along_axis(i1, si // PEER_TOPK, axis=-1)
    e2 = jnp.take_along_axis(i2, si % PEER_TOPK, axis=-1)
    experts = e1 * N_KEYS + e2
    gates = jax.nn.softmax(sv, axis=-1)
    n_act = PEER_HEADS * PEER_TOPK
    n_chunks = (B * S) // PEER_CHUNK
    hc = h.reshape(n_chunks, PEER_CHUNK, D)
    ec = experts.reshape(n_chunks, PEER_CHUNK, n_act)
    gc = gates.reshape(n_chunks, PEER_CHUNK, n_act)

    def chunk(args):
        h_c, e_c, g_c = args
        a = jnp.einsum('cd,ced->ce', h_c.astype(jnp.float32), u[e_c].astype(jnp.float32))
        act = jax.nn.gelu(a, approximate=False) * g_c
        return jnp.einsum('ce,ced->cd', act, v[e_c].astype(jnp.float32))

    out = lax.map(chunk, (hc, ec, gc))
    return out.reshape(B, S, D).astype(h.dtype)


def setup_inputs(seed: int = 0) -> dict:
    key = jax.random.key(seed)
    ks = jax.random.split(key, 25)
    L = DEPTH
    D = D_MODEL

    def nrm(k, shape, scale):
        return scale * jax.random.normal(k, shape, jnp.float32)

    return {
        'x': nrm(ks[0], (BATCH, SEQ, D), 1.0),
        'mem': nrm(ks[1], (BATCH, N_MEM, D), 1.0),
        'norm1_g': 1.0 + nrm(ks[2], (L, D), 0.02),
        'w_in': nrm(ks[3], (L, D, IN_COLS), D ** -0.5),
        'b_gate': nrm(ks[4], (L, N_BRANCH * D), 0.02),
        'b_forget': 2.0 + nrm(ks[5], (L, FOX_HEADS), 0.5),
        'fox_q_g': 1.0 + nrm(ks[6], (L, FOX_HD), 0.02),
        'fox_k_g': 1.0 + nrm(ks[7], (L, FOX_HD), 0.02),
        'swa_q_g': 1.0 + nrm(ks[8], (L, SWA_HD), 0.02),
        'swa_k_g': 1.0 + nrm(ks[9], (L, SWA_HD), 0.02),
        'swa_sinks': nrm(ks[10], (L, SWA_HEADS), 0.5),
        'mem_norm_g': 1.0 + nrm(ks[11], (L, D), 0.02),
        'w_mem_kv': nrm(ks[12], (L, D, 2 * MEM_W), D ** -0.5),
        'mem_q_g': 1.0 + nrm(ks[13], (L, MEM_HD), 0.02),
        'mem_k_g': 1.0 + nrm(ks[14], (L, MEM_HD), 0.02),
        'w_fox_o': nrm(ks[15], (L, FOX_W, D), FOX_W ** -0.5),
        'w_swa_o': nrm(ks[16], (L, SWA_W, D), SWA_W ** -0.5),
        'w_mem_o': nrm(ks[17], (L, MEM_W, D), MEM_W ** -0.5),
        'w_out': nrm(ks[18], (L, D, D), D ** -0.5),
        'norm2_g': 1.0 + nrm(ks[19], (L, D), 0.02),
        'w_peer_q': nrm(ks[20], (L, D, PEER_HEADS * PEER_QD), D ** -0.5),
        'peer_keys1': nrm(ks[21], (L, PEER_HEADS, N_KEYS, PEER_QD // 2), (PEER_QD // 2) ** -0.5),
        'peer_keys2': nrm(ks[22], (L, PEER_HEADS, N_KEYS, PEER_QD // 2), (PEER_QD // 2) ** -0.5),
        'peer_u': nrm(ks[23], (L, N_EXPERTS, D), D ** -0.5),
        'peer_v': nrm(ks[24], (L, N_EXPERTS, D), (PEER_HEADS * PEER_TOPK) ** -0.5),
    }


def reference(x, mem, norm1_g, w_in, b_gate, b_forget, fox_q_g, fox_k_g, swa_q_g, swa_k_g,
              swa_sinks, mem_norm_g, w_mem_kv, mem_q_g, mem_k_g, w_fox_o, w_swa_o, w_mem_o,
              w_out, norm2_g, w_peer_q, peer_keys1, peer_keys2, peer_u, peer_v):
    B, S, D = x.shape
    M = mem.shape[1]
    splits = [int(c) for c in np.cumsum(IN_WIDTHS)[:-1]]
    slopes = alibi_slopes(SWA_HEADS)
    for l in range(DEPTH):
        h = rmsnorm(x, norm1_g[l])
        z = h @ w_in[l]
        fq, fk, fv, ff, sq, sk, sv, mq, gl = jnp.split(z, splits, axis=-1)

        fq = head_rmsnorm(fq.reshape(B, S, FOX_HEADS, FOX_HD), fox_q_g[l])
        fk = head_rmsnorm(fk.reshape(B, S, FOX_HEADS, FOX_HD), fox_k_g[l])
        log_f = jax.nn.log_sigmoid((ff + b_forget[l]).astype(jnp.float32))
        o_fox = fox_attention(fq, fk, fv.reshape(B, S, FOX_HEADS, FOX_HD), log_f)
        o_fox = o_fox.reshape(B, S, FOX_W).astype(x.dtype)

        sq = head_rmsnorm(sq.reshape(B, S, SWA_HEADS, SWA_HD), swa_q_g[l])
        sk = head_rmsnorm(sk.reshape(B, S, SWA_KV_HEADS, SWA_HD), swa_k_g[l])
        o_swa = swa_attention(sq, sk, sv.reshape(B, S, SWA_KV_HEADS, SWA_HD), swa_sinks[l], slopes)
        o_swa = o_swa.reshape(B, S, SWA_W).astype(x.dtype)

        mn = rmsnorm(mem, mem_norm_g[l])
        mk, mv = jnp.split(mn @ w_mem_kv[l], 2, axis=-1)
        mk = head_rmsnorm(mk.reshape(B, M, MEM_HEADS, MEM_HD), mem_k_g[l])
        mq = head_rmsnorm(mq.reshape(B, S, MEM_HEADS, MEM_HD), mem_q_g[l])
        o_mem = memory_attention(mq, mk, mv.reshape(B, M, MEM_HEADS, MEM_HD))
        o_mem = o_mem.reshape(B, S, MEM_W).astype(x.dtype)

        gates = jax.nn.sigmoid((gl + b_gate[l]).astype(jnp.float32)).astype(x.dtype)
        gates = gates.reshape(B, S, N_BRANCH, D)
        merged = (gates[:, :, 0] * (o_fox @ w_fox_o[l])
                  + gates[:, :, 1] * (o_swa @ w_swa_o[l])
                  + gates[:, :, 2] * (o_mem @ w_mem_o[l]))
        x = x + merged @ w_out[l]

        x = x + peer_ffn(rmsnorm(x, norm2_g[l]), w_peer_q[l], peer_keys1[l], peer_keys2[l],
                         peer_u[l], peer_v[l])
    return x
```

```python
import functools

import jax
import jax.numpy as jnp
from jax import lax
from jax.experimental import pallas as pl
from jax.experimental.pallas import tpu as pltpu

F32 = jnp.float32
BF16 = jnp.bfloat16

D_MODEL = 1024
BLOCK = 128
FOX_HEADS = 8
HEAD_D = 64
SWA_HEADS = 8
SWA_KV_HEADS = 2
MEM_HEADS = 4
MEM_HD = 128
N_MEM = 256
PEER_HEADS = 8
N_KEYS = 128
N_EXPERTS = N_KEYS * N_KEYS
PEER_TOPK = 16
N_ACT = PEER_HEADS * PEER_TOPK
EPS = 1e-6
NEG_INF = -1e30

LANES = 128
ROWS_PER_EXPERT = 4
VMEM_LIMIT = 48 * 1024 * 1024

C_FQ, C_FK, C_FV, C_SQ, C_SK, C_SV, C_MQ, C_FF = 0, 512, 1024, 1536, 2048, 2176, 2304, 2816
ATT_COLS = 2944


def _cparams(sem):
    return pltpu.CompilerParams(dimension_semantics=sem, vmem_limit_bytes=VMEM_LIMIT)


def _whole():
    return pl.BlockSpec(memory_space=pltpu.VMEM)


def _rms(xf, g):
    return xf * lax.rsqrt(jnp.mean(xf * xf, axis=-1, keepdims=True) + EPS) * g


def _seg_matrix(seg):
    r = lax.broadcasted_iota(jnp.int32, (LANES, LANES), 0) // seg
    c = lax.broadcasted_iota(jnp.int32, (LANES, LANES), 1) // seg
    return jnp.where(r == c, 1.0 / seg, 0.0).astype(BF16)


def _seg_norm(z, seg_mat, gain):
    sq = z * z
    hi = sq.astype(BF16)
    lo = (sq - hi.astype(F32)).astype(BF16)
    ms = jnp.dot(hi, seg_mat, preferred_element_type=F32) + jnp.dot(lo, seg_mat, preferred_element_type=F32)
    return z * lax.rsqrt(ms + EPS) * gain


def _log_sigmoid(x):
    return jnp.minimum(x, 0.0) - jnp.log1p(jnp.exp(-jnp.abs(x)))


def _sigmoid(x):
    return 1.0 / (1.0 + jnp.exp(-x))


def _gelu(x):
    return 0.5 * x * (1.0 + lax.erf(x * 0.7071067811865476))


def _inproj_kernel(x_ref, g1_ref, w_ref, gains_ref, bf_ref,
                   fq_ref, fk_ref, fv_ref, sq_ref, sk_ref, sv_ref, mq_ref, lf_ref):
    h = _rms(x_ref[...], g1_ref[...]).astype(BF16)
    seg64 = _seg_matrix(HEAD_D)
    seg128 = _seg_matrix(MEM_HD)

    def proj(c0, width):
        return jnp.dot(h, w_ref[:, c0:c0 + width], preferred_element_type=F32)

    def normed(c0, nblk, seg_mat, gain_row, out_ref):
        gain = gains_ref[gain_row:gain_row + 1, :]
        for b in range(nblk):
            z = proj(c0 + b * LANES, LANES)
            out_ref[:, b * LANES:(b + 1) * LANES] = _seg_norm(z, seg_mat, gain).astype(BF16)

    normed(C_FQ, 4, seg64, 0, fq_ref)
    normed(C_FK, 4, seg64, 1, fk_ref)
    fv_ref[...] = proj(C_FV, 512).astype(BF16)
    normed(C_SQ, 4, seg64, 2, sq_ref)
    normed(C_SK, 1, seg64, 3, sk_ref)
    sv_ref[...] = proj(C_SV, 128).astype(BF16)
    normed(C_MQ, 4, seg128, 4, mq_ref)
    lf_ref[...] = _log_sigmoid(proj(C_FF, 128) + bf_ref[...])


def _inproj(xf, g1, w_att, gains, bf, tm):
    n = xf.shape[0]
    row = lambda w: pl.BlockSpec((tm, w), lambda i: (i, 0))
    out_shape = [jax.ShapeDtypeStruct((n, w), BF16) for w in (512, 512, 512, 512, 128, 128, 512)]
    out_shape.append(jax.ShapeDtypeStruct((n, 128), F32))
    return pl.pallas_call(
        _inproj_kernel,
        grid=(n // tm,),
        in_specs=[row(D_MODEL), _whole(), _whole(), _whole(), _whole()],
        out_specs=[row(512), row(512), row(512), row(512), row(128), row(128), row(512), row(128)],
        out_shape=out_shape,
        compiler_params=_cparams(("parallel",)),
        name="inproj",
    )(xf, g1, w_att, gains, bf)


def _cumsum_kernel(lf_ref, ccol_ref, crow_ref, *, blk):
    s = lf_ref.shape[0]
    r = lax.broadcasted_iota(jnp.int32, (blk, blk), 0)
    c = lax.broadcasted_iota(jnp.int32, (blk, blk), 1)
    tri = jnp.where(c <= r, 1.0, 0.0).astype(F32)
    carry = jnp.zeros((1, LANES), F32)
    for b in range(s // blk):
        x = lf_ref[b * blk:(b + 1) * blk, :]
        cs = jnp.dot(tri, x, preferred_element_type=F32, precision=lax.Precision.HIGHEST) + carry
        ccol_ref[b * blk:(b + 1) * blk, :] = cs
        carry = cs[blk - 1:blk, :]
    ct = jnp.transpose(ccol_ref[...])
    crow_ref[...] = ct[0:FOX_HEADS, :]


def _cumsum(lf3):
    b, s, _ = lf3.shape
    return pl.pallas_call(
        functools.partial(_cumsum_kernel, blk=256),
        grid=(b,),
        in_specs=[pl.BlockSpec((None, s, LANES), lambda i: (i, 0, 0))],
        out_specs=[pl.BlockSpec((None, s, LANES), lambda i: (i, 0, 0)),
                   pl.BlockSpec((None, FOX_HEADS, s), lambda i: (i, 0, 0))],
        out_shape=[jax.ShapeDtypeStruct((b, s, LANES), F32), jax.ShapeDtypeStruct((b, FOX_HEADS, s), F32)],
        compiler_params=_cparams(("parallel",)),
        name="cumsum",
    )(lf3)


def _fox_kernel(q_ref, k_ref, v_ref, ccol_ref, crow0_ref, crow1_ref, o_ref, m_scr, l_scr, acc_scr, *, tq):
    hp = pl.program_id(1)
    qi = pl.program_id(2)
    lane = lax.broadcasted_iota(jnp.int32, (tq, LANES), 1)
    q = q_ref[...]
    zero = jnp.zeros_like(q)
    qh = (jnp.where(lane < HEAD_D, q, zero), jnp.where(lane >= HEAD_D, q, zero))
    cc = ccol_ref[...]
    ci = tuple(jnp.sum(jnp.where(lane == 2 * hp + a, cc, 0.0), axis=1, keepdims=True) for a in range(2))
    crow = (crow0_ref, crow1_ref)
    m_scr[...] = jnp.full(m_scr.shape, NEG_INF, F32)
    l_scr[...] = jnp.zeros(l_scr.shape, F32)
    acc_scr[...] = jnp.zeros(acc_scr.shape, F32)
    rows = lax.broadcasted_iota(jnp.int32, (tq, tq), 0)
    cols = lax.broadcasted_iota(jnp.int32, (tq, tq), 1)

    def step(j, masked):
        ks = pl.multiple_of(j * tq, tq)
        kb = k_ref[pl.ds(ks, tq), :]
        vb = v_ref[pl.ds(ks, tq), :]
        for a in range(2):
            s = lax.dot_general(qh[a], kb, (((1,), (1,)), ((), ())), preferred_element_type=F32)
            s = s + (ci[a] - crow[a][:, pl.ds(ks, tq)])
            if masked:
                s = jnp.where(cols <= rows, s, NEG_INF)
            m_prev = m_scr[a]
            m_next = jnp.maximum(m_prev, jnp.max(s, axis=1, keepdims=True))
            p = jnp.exp(s - jnp.concatenate([m_next] * (tq // LANES), axis=1))
            alpha = jnp.exp(m_prev - m_next)
            l_scr[a] = alpha * l_scr[a] + jnp.sum(p, axis=1, keepdims=True)
            acc_scr[a] = alpha * acc_scr[a] + jnp.dot(p.astype(BF16), vb, preferred_element_type=F32)
            m_scr[a] = m_next

    def body(j, c):
        step(j, False)
        return c

    lax.fori_loop(0, qi, body, 0)
    step(qi, True)
    o0 = acc_scr[0] / l_scr[0]
    o1 = acc_scr[1] / l_scr[1]
    o_ref[...] = jnp.where(lane < HEAD_D, o0, o1).astype(BF16)


def _fox(fq, fk, fv, ccol, crow, b, s, tq):
    nq = s // tq
    qspec = pl.BlockSpec((tq, LANES), lambda bi, hp, qi: (bi * nq + qi, hp))
    kvspec = pl.BlockSpec((s, LANES), lambda bi, hp, qi: (bi, hp))
    return pl.pallas_call(
        functools.partial(_fox_kernel, tq=tq),
        grid=(b, FOX_HEADS // 2, nq),
        in_specs=[qspec, kvspec, kvspec,
                  pl.BlockSpec((tq, LANES), lambda bi, hp, qi: (bi * nq + qi, 0)),
                  pl.BlockSpec((None, 1, s), lambda bi, hp, qi: (bi * FOX_HEADS + 2 * hp, 0, 0)),
                  pl.BlockSpec((None, 1, s), lambda bi, hp, qi: (bi * FOX_HEADS + 2 * hp + 1, 0, 0))],
        out_specs=qspec,
        out_shape=jax.ShapeDtypeStruct(fq.shape, BF16),
        scratch_shapes=[pltpu.VMEM((2, tq, LANES), F32), pltpu.VMEM((2, tq, LANES), F32),
                        pltpu.VMEM((2, tq, LANES), F32)],
        compiler_params=_cparams(("parallel", "parallel", "arbitrary")),
        name="fox",
    )(fq, fk, fv, ccol, crow, crow)


def _swa_kernel(sink_ref, q_ref, kp_ref, kc_ref, vp_ref, vc_ref, o_ref):
    blk = pl.program_id(1)
    lane = lax.broadcasted_iota(jnp.int32, (BLOCK, LANES), 1)
    kb = jnp.concatenate([kp_ref[...], kc_ref[...]], axis=0)
    vb = jnp.concatenate([vp_ref[...], vc_ref[...]], axis=0)
    qi = lax.broadcasted_iota(jnp.int32, (BLOCK, 2 * BLOCK), 0)
    kj = lax.broadcasted_iota(jnp.int32, (BLOCK, 2 * BLOCK), 1)
    dist = qi + BLOCK - kj
    ok = (dist >= 0) & (dist < BLOCK) & ((blk > 0) | (kj >= BLOCK))
    distf = dist.astype(F32)
    rep = SWA_HEADS // SWA_KV_HEADS
    for pair in range(SWA_HEADS // 2):
        q128 = q_ref[:, pair * LANES:(pair + 1) * LANES]
        outs = []
        for half in range(2):
            h = 2 * pair + half
            grp = h // rep
            qa = q128 if half == grp else pltpu.roll(q128, HEAD_D, axis=1)
            in_grp = (lane >= grp * HEAD_D) & (lane < (grp + 1) * HEAD_D)
            qa = jnp.where(in_grp, qa, jnp.zeros_like(qa))
            s = lax.dot_general(qa, kb, (((1,), (1,)), ((), ())), preferred_element_type=F32)
            s = s - (2.0 ** -(h + 1)) * distf
            s = jnp.where(ok, s, NEG_INF)
            sink = sink_ref[h]
            m = jnp.maximum(jnp.max(s, axis=1, keepdims=True), sink)
            p = jnp.exp(s - m)
            den = jnp.sum(p, axis=1, keepdims=True) + jnp.exp(sink - m)
            o = jnp.dot(p.astype(BF16), vb, preferred_element_type=F32) / den
            outs.append(o if half == grp else pltpu.roll(o, HEAD_D, axis=1))
        o_ref[:, pair * LANES:(pair + 1) * LANES] = jnp.where(lane < HEAD_D, outs[0], outs[1]).astype(BF16)


def _swa(sinks, sq, sk, sv, b, s):
    nb = s // BLOCK
    cur = lambda bi, i: (bi * nb + i, 0)
    prev = lambda bi, i: (bi * nb + jnp.maximum(i - 1, 0), 0)
    return pl.pallas_call(
        _swa_kernel,
        grid=(b, nb),
        in_specs=[pl.BlockSpec(memory_space=pltpu.SMEM),
                  pl.BlockSpec((BLOCK, 512), cur),
                  pl.BlockSpec((BLOCK, LANES), prev), pl.BlockSpec((BLOCK, LANES), cur),
                  pl.BlockSpec((BLOCK, LANES), prev), pl.BlockSpec((BLOCK, LANES), cur)],
        out_specs=pl.BlockSpec((BLOCK, 512), cur),
        out_shape=jax.ShapeDtypeStruct(sq.shape, BF16),
        compiler_params=_cparams(("parallel", "parallel")),
        name="swa",
    )(sinks, sq, sk, sk, sv, sv)


def _memkv_kernel(mem_ref, g_ref, w_ref, gk_ref, mk_ref, mv_ref):
    mn = _rms(mem_ref[...], g_ref[...]).astype(BF16)
    seg128 = _seg_matrix(MEM_HD)
    width = MEM_HEADS * MEM_HD
    for hd in range(MEM_HEADS):
        z = jnp.dot(mn, w_ref[:, hd * LANES:(hd + 1) * LANES], preferred_element_type=F32)
        mk_ref[:, hd * LANES:(hd + 1) * LANES] = _seg_norm(z, seg128, gk_ref[...]).astype(BF16)
    mv_ref[...] = jnp.dot(mn, w_ref[:, width:2 * width], preferred_element_type=F32).astype(BF16)


def _memkv(memf, g, w, gk):
    n = memf.shape[0]
    width = MEM_HEADS * MEM_HD
    tm = N_MEM
    return pl.pallas_call(
        _memkv_kernel,
        grid=(n // tm,),
        in_specs=[pl.BlockSpec((tm, D_MODEL), lambda i: (i, 0)), _whole(), _whole(), _whole()],
        out_specs=[pl.BlockSpec((tm, width), lambda i: (i, 0)), pl.BlockSpec((tm, width), lambda i: (i, 0))],
        out_shape=[jax.ShapeDtypeStruct((n, width), BF16), jax.ShapeDtypeStruct((n, width), BF16)],
        compiler_params=_cparams(("parallel",)),
        name="memkv",
    )(memf, g, w, gk)


def _mem_kernel(q_ref, k_ref, v_ref, o_ref):
    for hd in range(MEM_HEADS):
        sl = slice(hd * LANES, (hd + 1) * LANES)
        s = lax.dot_general(q_ref[:, sl], k_ref[:, sl], (((1,), (1,)), ((), ())), preferred_element_type=F32)
        m = jnp.max(s, axis=1, keepdims=True)
        p = jnp.exp(s - m)
        den = jnp.sum(p, axis=1, keepdims=True)
        o_ref[:, sl] = (jnp.dot(p.astype(BF16), v_ref[:, sl], preferred_element_type=F32) / den).astype(BF16)


def _mem(mq, mk, mv, b, s, tq):
    nq = s // tq
    width = MEM_HEADS * MEM_HD
    qspec = pl.BlockSpec((tq, width), lambda bi, qi: (bi * nq + qi, 0))
    kvspec = pl.BlockSpec((N_MEM, width), lambda bi, qi: (bi, 0))
    return pl.pallas_call(
        _mem_kernel,
        grid=(b, nq),
        in_specs=[qspec, kvspec, kvspec],
        out_specs=qspec,
        out_shape=jax.ShapeDtypeStruct(mq.shape, BF16),
        compiler_params=_cparams(("parallel", "parallel")),
        name="mem",
    )(mq, mk, mv)


def _merge_kernel(x_ref, of_ref, os_ref, om_ref, g1_ref, wg_ref, bg_ref, wfo_ref, wso_ref, wmo_ref, wout_ref,
                  g2_ref, wpq_ref, x1_ref, h2_ref, qp_ref):
    x = x_ref[...]
    h = _rms(x, g1_ref[...]).astype(BF16)
    merged = jnp.zeros(x.shape, F32)
    for br, (o_ref, w_ref) in enumerate(((of_ref, wfo_ref), (os_ref, wso_ref), (om_ref, wmo_ref))):
        sl = slice(br * D_MODEL, (br + 1) * D_MODEL)
        gate = _sigmoid(jnp.dot(h, wg_ref[:, sl], preferred_element_type=F32) + bg_ref[:, sl])
        merged = merged + gate * jnp.dot(o_ref[...], w_ref[...], preferred_element_type=F32)
    x1 = x + jnp.dot(merged.astype(BF16), wout_ref[...], preferred_element_type=F32)
    x1_ref[...] = x1
    h2 = _rms(x1, g2_ref[...])
    h2_ref[...] = h2
    qp_ref[...] = jnp.dot(h2.astype(BF16), wpq_ref[...], preferred_element_type=F32).astype(BF16)


def _merge(xf, of, os_, om, g1, wg, bg, wfo, wso, wmo, wout, g2, wpq, tm):
    n = xf.shape[0]
    row = lambda w: pl.BlockSpec((tm, w), lambda i: (i, 0))
    qw = wpq.shape[1]
    return pl.pallas_call(
        _merge_kernel,
        grid=(n // tm,),
        in_specs=[row(D_MODEL), row(512), row(512), row(512)] + [_whole()] * 9,
        out_specs=[row(D_MODEL), row(D_MODEL), row(qw)],
        out_shape=[jax.ShapeDtypeStruct((n, D_MODEL), F32), jax.ShapeDtypeStruct((n, D_MODEL), F32),
                   jax.ShapeDtypeStruct((n, qw), BF16)],
        compiler_params=_cparams(("parallel",)),
        name="merge",
    )(xf, of, os_, om, g1, wg, bg, wfo, wso, wmo, wout, g2, wpq)


def _topk_rows(s, k, payload=None):
    nrow = s.shape[0]
    rid = lax.broadcasted_iota(jnp.int32, s.shape, 0)
    vals, sels = [], []
    for _ in range(k):
        m = jnp.max(s, axis=0, keepdims=True)
        sel = jnp.min(jnp.where(s == m, rid, nrow), axis=0, keepdims=True)
        hit = rid == sel
        vals.append(m)
        if payload is None:
            sels.append(sel)
        else:
            sels.append(jnp.max(jnp.where(hit, payload, -1), axis=0, keepdims=True))
        s = jnp.where(hit, -jnp.inf, s)
    return jnp.concatenate(vals, axis=0), jnp.concatenate(sels, axis=0)


def _topk_kernel(qp_ref, k1_ref, k2_ref, e_ref, g_ref):
    nt = (((1,), (1,)), ((), ()))
    for hd in range(PEER_HEADS):
        c0 = hd * 2 * N_KEYS
        s1 = lax.dot_general(k1_ref[hd], qp_ref[:, c0:c0 + N_KEYS], nt, preferred_element_type=F32)
        s2 = lax.dot_general(k2_ref[hd], qp_ref[:, c0 + N_KEYS:c0 + 2 * N_KEYS], nt, preferred_element_type=F32)
        v1, i1 = _topk_rows(s1, PEER_TOPK)
        v2, i2 = _topk_rows(s2, PEER_TOPK)
        cand = jnp.concatenate([v1[a:a + 1, :] + v2 for a in range(PEER_TOPK)], axis=0)
        eidx = jnp.concatenate([i1[a:a + 1, :] * N_KEYS + i2 for a in range(PEER_TOPK)], axis=0)
        sv, ex = _topk_rows(cand, PEER_TOPK, payload=eidx)
        p = jnp.exp(sv - sv[0:1, :])
        g_ref[hd * PEER_TOPK:(hd + 1) * PEER_TOPK, :] = p / jnp.sum(p, axis=0, keepdims=True)
        e_ref[hd * PEER_TOPK:(hd + 1) * PEER_TOPK, :] = ex


def _topk(qp, k1, k2, tb):
    n = qp.shape[0]
    nb = n // tb
    oblk = pl.BlockSpec((None, N_ACT, tb), lambda i: (i, 0, 0))
    return pl.pallas_call(
        _topk_kernel,
        grid=(nb,),
        in_specs=[pl.BlockSpec((tb, qp.shape[1]), lambda i: (i, 0)), _whole(), _whole()],
        out_specs=[oblk, oblk],
        out_shape=[jax.ShapeDtypeStruct((nb, N_ACT, tb), jnp.int32), jax.ShapeDtypeStruct((nb, N_ACT, tb), F32)],
        compiler_params=_cparams(("parallel",)),
        name="topk",
    )(qp, k1, k2)


def _pack_table(w):
    wb = w.astype(BF16)
    half = w.shape[1] // 2
    lo = lax.bitcast_convert_type(wb[:, :half], jnp.uint16).astype(jnp.uint32)
    hi = lax.bitcast_convert_type(wb[:, half:], jnp.uint16).astype(jnp.uint32)
    pk = lax.bitcast_convert_type(lo | (hi << 16), jnp.int32)
    return pk.reshape(w.shape[0] * ROWS_PER_EXPERT, LANES)


def _unpack(w):
    lo = pltpu.bitcast(w << 16, F32)
    hi = pltpu.bitcast(w & jnp.int32(-65536), F32)
    return lo, hi


def _gather_rows(idx_ref, tab_ref, g_ref, t):
    for j in range(N_ACT):
        e4 = pl.multiple_of(idx_ref[t, j], ROWS_PER_EXPERT)
        g_ref[ROWS_PER_EXPERT * j:ROWS_PER_EXPERT * (j + 1), :] = tab_ref[pl.ds(e4, ROWS_PER_EXPERT), :]


def _pair_loop(tb, gather, compute, g0, g1):
    gather(0, g0)

    def body(tt, c):
        t0 = 2 * tt
        gather(t0 + 1, g1)
        compute(t0, g0)
        gather(jnp.minimum(t0 + 2, tb - 1), g0)
        compute(t0 + 1, g1)
        return c

    lax.fori_loop(0, tb // 2, body, 0)


def _gather_cb(idx_ref, tab_ref):
    return lambda t, g: _gather_rows(idx_ref, tab_ref, g, t)


def _peer_u(idx, h2, gates_t, upk, tb):
    nb = idx.shape[0]
    jblk = pl.BlockSpec((None, N_ACT, tb), lambda i: (i, 0, 0))

    def kern(idx_ref, h_ref, gate_ref, tab_ref, act_ref, g0, g1, at_ref):
        _peer_u_body(idx_ref, h_ref, gate_ref, tab_ref, act_ref, g0, g1, at_ref, tb)

    return pl.pallas_call(
        kern,
        grid=(nb,),
        in_specs=[pl.BlockSpec((None, tb, N_ACT), lambda i: (i, 0, 0), memory_space=pltpu.SMEM),
                  pl.BlockSpec((tb, D_MODEL), lambda i: (i, 0)), jblk, _whole()],
        out_specs=jblk,
        out_shape=jax.ShapeDtypeStruct((nb, N_ACT, tb), F32),
        scratch_shapes=[pltpu.VMEM((N_ACT * ROWS_PER_EXPERT, LANES), jnp.int32),
                        pltpu.VMEM((N_ACT * ROWS_PER_EXPERT, LANES), jnp.int32),
                        pltpu.VMEM((N_ACT, tb), F32)],
        compiler_params=_cparams(("arbitrary",)),
        name="peer_u",
    )(idx, h2, gates_t, upk)


def _peer_u_body(idx_ref, h_ref, gate_ref, tab_ref, act_ref, g0, g1, at_ref, tb):
    ones = jnp.ones((LANES, LANES), F32)
    lane = lax.broadcasted_iota(jnp.int32, (N_ACT, tb), 1)
    half = D_MODEL // 2

    def compute(t, g_ref):
        hrow = h_ref[pl.ds(t, 1), :]
        acc = jnp.zeros((N_ACT, LANES), F32)
        for s in range(ROWS_PER_EXPERT):
            lo, hi = _unpack(g_ref[pl.ds(s, N_ACT, stride=ROWS_PER_EXPERT), :])
            acc = acc + lo * hrow[:, s * LANES:(s + 1) * LANES] + hi * hrow[:, half + s * LANES:half + (s + 1) * LANES]
        srep = jnp.dot(acc, ones, preferred_element_type=F32, precision=lax.Precision.HIGHEST)
        at_ref[...] = jnp.where(lane == t, srep, at_ref[...])

    _pair_loop(tb, _gather_cb(idx_ref, tab_ref), compute, g0, g1)
    act_ref[...] = _gelu(at_ref[...]) * gate_ref[...]


def _peer_v_body(idx_ref, act_ref, x1_ref, tab_ref, out_ref, g0, g1, tb):
    rowid = lax.broadcasted_iota(jnp.int32, (tb, LANES), 0)
    half = D_MODEL // 2

    def compute(t, g_ref):
        onehot = jnp.where(rowid == t, 1.0, 0.0).astype(F32)
        arep = jnp.dot(act_ref[...], onehot, preferred_element_type=F32, precision=lax.Precision.HIGHEST)
        los, his = [], []
        for s in range(ROWS_PER_EXPERT):
            lo, hi = _unpack(g_ref[pl.ds(s, N_ACT, stride=ROWS_PER_EXPERT), :])
            los.append(jnp.sum(lo * arep, axis=0, keepdims=True))
            his.append(jnp.sum(hi * arep, axis=0, keepdims=True))
        out_ref[pl.ds(t, 1), :] = x1_ref[pl.ds(t, 1), :] + jnp.concatenate(los + his, axis=1)

    _pair_loop(tb, _gather_cb(idx_ref, tab_ref), compute, g0, g1)


def _peer_v(idx, act_t, x1, vpk, tb):
    nb = idx.shape[0]

    def kern(idx_ref, act_ref, x1_ref, tab_ref, out_ref, g0, g1):
        _peer_v_body(idx_ref, act_ref, x1_ref, tab_ref, out_ref, g0, g1, tb)

    return pl.pallas_call(
        kern,
        grid=(nb,),
        in_specs=[pl.BlockSpec((None, tb, N_ACT), lambda i: (i, 0, 0), memory_space=pltpu.SMEM),
                  pl.BlockSpec((None, N_ACT, tb), lambda i: (i, 0, 0)),
                  pl.BlockSpec((tb, D_MODEL), lambda i: (i, 0)), _whole()],
        out_specs=pl.BlockSpec((tb, D_MODEL), lambda i: (i, 0)),
        out_shape=jax.ShapeDtypeStruct(x1.shape, F32),
        scratch_shapes=[pltpu.VMEM((N_ACT * ROWS_PER_EXPERT, LANES), jnp.int32),
                        pltpu.VMEM((N_ACT * ROWS_PER_EXPERT, LANES), jnp.int32)],
        compiler_params=_cparams(("arbitrary",)),
        name="peer_v",
    )(idx, act_t, x1, vpk)


def _tile(n, pref):
    t = min(pref, n)
    while n % t:
        t //= 2
    return t


def kernel(x, mem, norm1_g, w_in, b_gate, b_forget, fox_q_g, fox_k_g, swa_q_g, swa_k_g, swa_sinks, mem_norm_g, w_mem_kv, mem_q_g, mem_k_g, w_fox_o, w_swa_o, w_mem_o, w_out, norm2_g, w_peer_q, peer_keys1, peer_keys2, peer_u, peer_v):
    b, s, d = x.shape
    n = b * s
    assert d == D_MODEL and s % 256 == 0 and mem.shape[1] == N_MEM
    depth = norm1_g.shape[0]
    xf = x.reshape(n, d)
    memf = mem.reshape(b * N_MEM, d)
    tb = BLOCK
    for l in range(depth):
        w = w_in[l]
        o_ff, o_sq, o_sk, o_sv, o_mq, o_gl = 1536, 1544, 2056, 2184, 2312, 2824
        w_att = jnp.concatenate(
            [w[:, 0:o_ff], w[:, o_sq:o_gl], w[:, o_ff:o_sq], jnp.zeros((d, LANES - FOX_HEADS), F32)],
            axis=1).astype(BF16)
        w_gate = w[:, o_gl:].astype(BF16)
        tile2 = lambda g: jnp.concatenate([g, g])
        gains = jnp.stack([tile2(fox_q_g[l]) * HEAD_D ** -0.5, tile2(fox_k_g[l]),
                           tile2(swa_q_g[l]) * HEAD_D ** -0.5, tile2(swa_k_g[l]),
                           mem_q_g[l] * MEM_HD ** -0.5] + [jnp.zeros((LANES,), F32)] * 3)
        bf = jnp.concatenate([b_forget[l], jnp.zeros((LANES - FOX_HEADS,), F32)]).reshape(1, LANES)
        g1 = norm1_g[l].reshape(1, d)

        fq, fk, fv, sq, sk, sv, mq, lf = _inproj(xf, g1, w_att, gains, bf, _tile(n, 512))
        ccol, crow = _cumsum(lf.reshape(b, s, LANES))
        o_fox = _fox(fq, fk, fv, ccol.reshape(n, LANES), crow.reshape(b * FOX_HEADS, 1, s), b, s, 256)
        o_swa = _swa(swa_sinks[l], sq, sk, sv, b, s)
        mk, mv = _memkv(memf, mem_norm_g[l].reshape(1, d), w_mem_kv[l].astype(BF16), mem_k_g[l].reshape(1, LANES))
        o_mem = _mem(mq, mk, mv, b, s, 256)

        x1, h2, qp = _merge(xf, o_fox, o_swa, o_mem, g1, w_gate, b_gate[l].reshape(1, 3 * d),
                            w_fox_o[l].astype(BF16), w_swa_o[l].astype(BF16), w_mem_o[l].astype(BF16),
                            w_out[l].astype(BF16), norm2_g[l].reshape(1, d), w_peer_q[l].astype(BF16), _tile(n, 256))

        e_t, gates_t = _topk(qp, peer_keys1[l].astype(BF16), peer_keys2[l].astype(BF16), tb)
        idx = jnp.swapaxes(e_t, 1, 2) * ROWS_PER_EXPERT
        act_t = _peer_u(idx, h2, gates_t, _pack_table(peer_u[l]), tb)
        xf = _peer_v(idx, act_t, x1, _pack_table(peer_v[l]), tb)
    return xf.reshape(b, s, d)
```

```python
import functools

import jax
import jax.numpy as jnp
from jax import lax
from jax.experimental import pallas as pl
from jax.experimental.pallas import tpu as pltpu

F32 = jnp.float32
BF16 = jnp.bfloat16

D_MODEL = 1024
BLOCK = 128
FOX_HEADS = 8
HEAD_D = 64
SWA_HEADS = 8
SWA_KV_HEADS = 2
MEM_HEADS = 4
MEM_HD = 128
N_MEM = 256
PEER_HEADS = 8
N_KEYS = 128
N_EXPERTS = N_KEYS * N_KEYS
PEER_TOPK = 16
N_ACT = PEER_HEADS * PEER_TOPK
EPS = 1e-6
NEG_INF = -1e30

LANES = 128
ROWS_PER_EXPERT = 4
VMEM_LIMIT = 48 * 1024 * 1024

C_FQ, C_FK, C_FV, C_SQ, C_SK, C_SV, C_MQ, C_FF = 0, 512, 1024, 1536, 2048, 2176, 2304, 2816
ATT_COLS = 2944


def _cparams(sem):
    return pltpu.CompilerParams(dimension_semantics=sem, vmem_limit_bytes=VMEM_LIMIT)


def _whole():
    return pl.BlockSpec(memory_space=pltpu.VMEM)


def _rms(xf, g):
    return xf * lax.rsqrt(jnp.mean(xf * xf, axis=-1, keepdims=True) + EPS) * g


def _seg_matrix(seg):
    r = lax.broadcasted_iota(jnp.int32, (LANES, LANES), 0) // seg
    c = lax.broadcasted_iota(jnp.int32, (LANES, LANES), 1) // seg
    return jnp.where(r == c, 1.0 / seg, 0.0).astype(BF16)


def _seg_norm(z, seg_mat, gain):
    sq = z * z
    hi = sq.astype(BF16)
    lo = (sq - hi.astype(F32)).astype(BF16)
    ms = jnp.dot(hi, seg_mat, preferred_element_type=F32) + jnp.dot(lo, seg_mat, preferred_element_type=F32)
    return z * lax.rsqrt(ms + EPS) * gain


def _log_sigmoid(x):
    return jnp.minimum(x, 0.0) - jnp.log1p(jnp.exp(-jnp.abs(x)))


def _sigmoid(x):
    return 1.0 / (1.0 + jnp.exp(-x))


def _gelu(x):
    return 0.5 * x * (1.0 + lax.erf(x * 0.7071067811865476))


def _inproj_kernel(x_ref, g1_ref, w_ref, gains_ref, bf_ref,
                   fq_ref, fk_ref, fv_ref, sq_ref, sk_ref, sv_ref, mq_ref, lf_ref):
    h = _rms(x_ref[...], g1_ref[...]).astype(BF16)
    seg64 = _seg_matrix(HEAD_D)
    seg128 = _seg_matrix(MEM_HD)

    def proj(c0, width):
        return jnp.dot(h, w_ref[:, c0:c0 + width], preferred_element_type=F32)

    def normed(c0, nblk, seg_mat, gain_row, out_ref):
        gain = gains_ref[gain_row:gain_row + 1, :]
        for b in range(nblk):
            z = proj(c0 + b * LANES, LANES)
            out_ref[:, b * LANES:(b + 1) * LANES] = _seg_norm(z, seg_mat, gain).astype(BF16)

    normed(C_FQ, 4, seg64, 0, fq_ref)
    normed(C_FK, 4, seg64, 1, fk_ref)
    fv_ref[...] = proj(C_FV, 512).astype(BF16)
    normed(C_SQ, 4, seg64, 2, sq_ref)
    normed(C_SK, 1, seg64, 3, sk_ref)
    sv_ref[...] = proj(C_SV, 128).astype(BF16)
    normed(C_MQ, 4, seg128, 4, mq_ref)
    lf_ref[...] = _log_sigmoid(proj(C_FF, 128) + bf_ref[...])


def _inproj(xf, g1, w_att, gains, bf, tm):
    n = xf.shape[0]
    row = lambda w: pl.BlockSpec((tm, w), lambda i: (i, 0))
    out_shape = [jax.ShapeDtypeStruct((n, w), BF16) for w in (512, 512, 512, 512, 128, 128, 512)]
    out_shape.append(jax.ShapeDtypeStruct((n, 128), F32))
    return pl.pallas_call(
        _inproj_kernel,
        grid=(n // tm,),
        in_specs=[row(D_MODEL), _whole(), _whole(), _whole(), _whole()],
        out_specs=[row(512), row(512), row(512), row(512), row(128), row(128), row(512), row(128)],
        out_shape=out_shape,
        compiler_params=_cparams(("parallel",)),
        name="inproj",
    )(xf, g1, w_att, gains, bf)


def _cumsum_kernel(lf_ref, ccol_ref, crow_ref, *, blk):
    s = lf_ref.shape[0]
    r = lax.broadcasted_iota(jnp.int32, (blk, blk), 0)
    c = lax.broadcasted_iota(jnp.int32, (blk, blk), 1)
    tri = jnp.where(c <= r, 1.0, 0.0).astype(F32)
    carry = jnp.zeros((1, LANES), F32)
    for b in range(s // blk):
        x = lf_ref[b * blk:(b + 1) * blk, :]
        cs = jnp.dot(tri, x, preferred_element_type=F32, precision=lax.Precision.HIGHEST) + carry
        ccol_ref[b * blk:(b + 1) * blk, :] = cs
        carry = cs[blk - 1:blk, :]
    ct = jnp.transpose(ccol_ref[...])
    crow_ref[...] = ct[0:FOX_HEADS, :]


def _cumsum(lf3):
    b, s, _ = lf3.shape
    return pl.pallas_call(
        functools.partial(_cumsum_kernel, blk=256),
        grid=(b,),
        in_specs=[pl.BlockSpec((None, s, LANES), lambda i: (i, 0, 0))],
        out_specs=[pl.BlockSpec((None, s, LANES), lambda i: (i, 0, 0)),
                   pl.BlockSpec((None, FOX_HEADS, s), lambda i: (i, 0, 0))],
        out_shape=[jax.ShapeDtypeStruct((b, s, LANES), F32), jax.ShapeDtypeStruct((b, FOX_HEADS, s), F32)],
        compiler_params=_cparams(("parallel",)),
        name="cumsum",
    )(lf3)


def _fox_kernel(q_ref, k_ref, v_ref, ccol_ref, crow0_ref, crow1_ref, o_ref, m_scr, l_scr, acc_scr, *, tq):
    hp = pl.program_id(1)
    qi = pl.program_id(2)
    lane = lax.broadcasted_iota(jnp.int32, (tq, LANES), 1)
    q = q_ref[...]
    zero = jnp.zeros_like(q)
    qh = (jnp.where(lane < HEAD_D, q, zero), jnp.where(lane >= HEAD_D, q, zero))
    cc = ccol_ref[...]
    ci = tuple(jnp.sum(jnp.where(lane == 2 * hp + a, cc, 0.0), axis=1, keepdims=True) for a in range(2))
    crow = (crow0_ref, crow1_ref)
    m_scr[...] = jnp.full(m_scr.shape, NEG_INF, F32)
    l_scr[...] = jnp.zeros(l_scr.shape, F32)
    acc_scr[...] = jnp.zeros(acc_scr.shape, F32)
    rows = lax.broadcasted_iota(jnp.int32, (tq, tq), 0)
    cols = lax.broadcasted_iota(jnp.int32, (tq, tq), 1)

    def step(j, masked):
        ks = pl.multiple_of(j * tq, tq)
        kb = k_ref[pl.ds(ks, tq), :]
        vb = v_ref[pl.ds(ks, tq), :]
        for a in range(2):
            s = lax.dot_general(qh[a], kb, (((1,), (1,)), ((), ())), preferred_element_type=F32)
            s = s + (ci[a] - crow[a][:, pl.ds(ks, tq)])
            if masked:
                s = jnp.where(cols <= rows, s, NEG_INF)
            m_prev = m_scr[a]
            m_next = jnp.maximum(m_prev, jnp.max(s, axis=1, keepdims=True))
            p = jnp.exp(s - jnp.concatenate([m_next] * (tq // LANES), axis=1))
            alpha = jnp.exp(m_prev - m_next)
            l_scr[a] = alpha * l_scr[a] + jnp.sum(p, axis=1, keepdims=True)
            acc_scr[a] = alpha * acc_scr[a] + jnp.dot(p.astype(BF16), vb, preferred_element_type=F32)
            m_scr[a] = m_next

    def body(j, c):
        step(j, False)
        return c

    lax.fori_loop(0, qi, body, 0)
    step(qi, True)
    o0 = acc_scr[0] / l_scr[0]
    o1 = acc_scr[1] / l_scr[1]
    o_ref[...] = jnp.where(lane < HEAD_D, o0, o1).astype(BF16)


def _fox(fq, fk, fv, ccol, crow, b, s, tq):
    nq = s // tq
    qspec = pl.BlockSpec((tq, LANES), lambda bi, hp, qi: (bi * nq + qi, hp))
    kvspec = pl.BlockSpec((s, LANES), lambda bi, hp, qi: (bi, hp))
    return pl.pallas_call(
        functools.partial(_fox_kernel, tq=tq),
        grid=(b, FOX_HEADS // 2, nq),
        in_specs=[qspec, kvspec, kvspec,
                  pl.BlockSpec((tq, LANES), lambda bi, hp, qi: (bi * nq + qi, 0)),
                  pl.BlockSpec((None, 1, s), lambda bi, hp, qi: (bi * FOX_HEADS + 2 * hp, 0, 0)),
                  pl.BlockSpec((None, 1, s), lambda bi, hp, qi: (bi * FOX_HEADS + 2 * hp + 1, 0, 0))],
        out_specs=qspec,
        out_shape=jax.ShapeDtypeStruct(fq.shape, BF16),
        scratch_shapes=[pltpu.VMEM((2, tq, LANES), F32), pltpu.VMEM((2, tq, LANES), F32),
                        pltpu.VMEM((2, tq, LANES), F32)],
        compiler_params=_cparams(("parallel", "parallel", "arbitrary")),
        name="fox",
    )(fq, fk, fv, ccol, crow, crow)


def _swa_kernel(sink_ref, q_ref, kp_ref, kc_ref, vp_ref, vc_ref, o_ref):
    blk = pl.program_id(1)
    lane = lax.broadcasted_iota(jnp.int32, (BLOCK, LANES), 1)
    kb = jnp.concatenate([kp_ref[...], kc_ref[...]], axis=0)
    vb = jnp.concatenate([vp_ref[...], vc_ref[...]], axis=0)
    qi = lax.broadcasted_iota(jnp.int32, (BLOCK, 2 * BLOCK), 0)
    kj = lax.broadcasted_iota(jnp.int32, (BLOCK, 2 * BLOCK), 1)
    dist = qi + BLOCK - kj
    ok = (dist >= 0) & (dist < BLOCK) & ((blk > 0) | (kj >= BLOCK))
    distf = dist.astype(F32)
    rep = SWA_HEADS // SWA_KV_HEADS
    for pair in range(SWA_HEADS // 2):
        q128 = q_ref[:, pair * LANES:(pair + 1) * LANES]
        outs = []
        for half in range(2):
            h = 2 * pair + half
            grp = h // rep
            qa = q128 if half == grp else pltpu.roll(q128, HEAD_D, axis=1)
            in_grp = (lane >= grp * HEAD_D) & (lane < (grp + 1) * HEAD_D)
            qa = jnp.where(in_grp, qa, jnp.zeros_like(qa))
            s = lax.dot_general(qa, kb, (((1,), (1,)), ((), ())), preferred_element_type=F32)
            s = s - (2.0 ** -(h + 1)) * distf
            s = jnp.where(ok, s, NEG_INF)
            sink = sink_ref[h]
            m = jnp.maximum(jnp.max(s, axis=1, keepdims=True), sink)
            p = jnp.exp(s - m)
            den = jnp.sum(p, axis=1, keepdims=True) + jnp.exp(sink - m)
            o = jnp.dot(p.astype(BF16), vb, preferred_element_type=F32) / den
            outs.append(o if half == grp else pltpu.roll(o, HEAD_D, axis=1))
        o_ref[:, pair * LANES:(pair + 1) * LANES] = jnp.where(lane < HEAD_D, outs[0], outs[1]).astype(BF16)


def _swa(sinks, sq, sk, sv, b, s):
    nb = s // BLOCK
    cur = lambda bi, i: (bi * nb + i, 0)
    prev = lambda bi, i: (bi * nb + jnp.maximum(i - 1, 0), 0)
    return pl.pallas_call(
        _swa_kernel,
        grid=(b, nb),
        in_specs=[pl.BlockSpec(memory_space=pltpu.SMEM),
                  pl.BlockSpec((BLOCK, 512), cur),
                  pl.BlockSpec((BLOCK, LANES), prev), pl.BlockSpec((BLOCK, LANES), cur),
                  pl.BlockSpec((BLOCK, LANES), prev), pl.BlockSpec((BLOCK, LANES), cur)],
        out_specs=pl.BlockSpec((BLOCK, 512), cur),
        out_shape=jax.ShapeDtypeStruct(sq.shape, BF16),
        compiler_params=_cparams(("parallel", "parallel")),
        name="swa",
    )(sinks, sq, sk, sk, sv, sv)


def _memkv_kernel(mem_ref, g_ref, w_ref, gk_ref, mk_ref, mv_ref):
    mn = _rms(mem_ref[...], g_ref[...]).astype(BF16)
    seg128 = _seg_matrix(MEM_HD)
    width = MEM_HEADS * MEM_HD
    for hd in range(MEM_HEADS):
        z = jnp.dot(mn, w_ref[:, hd * LANES:(hd + 1) * LANES], preferred_element_type=F32)
        mk_ref[:, hd * LANES:(hd + 1) * LANES] = _seg_norm(z, seg128, gk_ref[...]).astype(BF16)
    mv_ref[...] = jnp.dot(mn, w_ref[:, width:2 * width], preferred_element_type=F32).astype(BF16)


def _memkv(memf, g, w, gk):
    n = memf.shape[0]
    width = MEM_HEADS * MEM_HD
    tm = N_MEM
    return pl.pallas_call(
        _memkv_kernel,
        grid=(n // tm,),
        in_specs=[pl.BlockSpec((tm, D_MODEL), lambda i: (i, 0)), _whole(), _whole(), _whole()],
        out_specs=[pl.BlockSpec((tm, width), lambda i: (i, 0)), pl.BlockSpec((tm, width), lambda i: (i, 0))],
        out_shape=[jax.ShapeDtypeStruct((n, width), BF16), jax.ShapeDtypeStruct((n, width), BF16)],
        compiler_params=_cparams(("parallel",)),
        name="memkv",
    )(memf, g, w, gk)


def _mem_kernel(q_ref, k_ref, v_ref, o_ref):
    for hd in range(MEM_HEADS):
        sl = slice(hd * LANES, (hd + 1) * LANES)
        s = lax.dot_general(q_ref[:, sl], k_ref[:, sl], (((1,), (1,)), ((), ())), preferred_element_type=F32)
        m = jnp.max(s, axis=1, keepdims=True)
        p = jnp.exp(s - m)
        den = jnp.sum(p, axis=1, keepdims=True)
        o_ref[:, sl] = (jnp.dot(p.astype(BF16), v_ref[:, sl], preferred_element_type=F32) / den).astype(BF16)


def _mem(mq, mk, mv, b, s, tq):
    nq = s // tq
    width = MEM_HEADS * MEM_HD
    qspec = pl.BlockSpec((tq, width), lambda bi, qi: (bi * nq + qi, 0))
    kvspec = pl.BlockSpec((N_MEM, width), lambda bi, qi: (bi, 0))
    return pl.pallas_call(
        _mem_kernel,
        grid=(b, nq),
        in_specs=[qspec, kvspec, kvspec],
        out_specs=qspec,
        out_shape=jax.ShapeDtypeStruct(mq.shape, BF16),
        compiler_params=_cparams(("parallel", "parallel")),
        name="mem",
    )(mq, mk, mv)


def _merge_kernel(x_ref, of_ref, os_ref, om_ref, g1_ref, wg_ref, bg_ref, wfo_ref, wso_ref, wmo_ref, wout_ref,
                  g2_ref, wpq_ref, x1_ref, h2_ref, qp_ref):
    x = x_ref[...]
    h = _rms(x, g1_ref[...]).astype(BF16)
    merged = jnp.zeros(x.shape, F32)
    for br, (o_ref, w_ref) in enumerate(((of_ref, wfo_ref), (os_ref, wso_ref), (om_ref, wmo_ref))):
        sl = slice(br * D_MODEL, (br + 1) * D_MODEL)
        gate = _sigmoid(jnp.dot(h, wg_ref[:, sl], preferred_element_type=F32) + bg_ref[:, sl])
        merged = merged + gate * jnp.dot(o_ref[...], w_ref[...], preferred_element_type=F32)
    x1 = x + jnp.dot(merged.astype(BF16), wout_ref[...], preferred_element_type=F32)
    x1_ref[...] = x1
    h2 = _rms(x1, g2_ref[...])
    h2_ref[...] = h2
    qp_ref[...] = jnp.dot(h2.astype(BF16), wpq_ref[...], preferred_element_type=F32).astype(BF16)


def _merge(xf, of, os_, om, g1, wg, bg, wfo, wso, wmo, wout, g2, wpq, tm):
    n = xf.shape[0]
    row = lambda w: pl.BlockSpec((tm, w), lambda i: (i, 0))
    qw = wpq.shape[1]
    return pl.pallas_call(
        _merge_kernel,
        grid=(n // tm,),
        in_specs=[row(D_MODEL), row(512), row(512), row(512)] + [_whole()] * 9,
        out_specs=[row(D_MODEL), row(D_MODEL), row(qw)],
        out_shape=[jax.ShapeDtypeStruct((n, D_MODEL), F32), jax.ShapeDtypeStruct((n, D_MODEL), F32),
                   jax.ShapeDtypeStruct((n, qw), BF16)],
        compiler_params=_cparams(("parallel",)),
        name="merge",
    )(xf, of, os_, om, g1, wg, bg, wfo, wso, wmo, wout, g2, wpq)


def _topk_rows(s, k, order, payload=None):
    big = jnp.int32(2 ** 30)
    vals, sels = [], []
    for _ in range(k):
        m = jnp.max(s, axis=0, keepdims=True)
        sel = jnp.min(jnp.where(s == m, order, big), axis=0, keepdims=True)
        hit = order == sel
        vals.append(m)
        if payload is None:
            sels.append(sel)
        else:
            sels.append(jnp.max(jnp.where(hit, payload, -1), axis=0, keepdims=True))
        s = jnp.where(hit, -jnp.inf, s)
    return jnp.concatenate(vals, axis=0), jnp.concatenate(sels, axis=0)


def _pair_candidates(v1, i1, v2, i2):
    t = v1.shape[1]
    sub = lax.broadcasted_iota(jnp.int32, (8, t), 0)
    low = sub < 4
    b_lo = jnp.where(low, sub, sub - 4)
    v2a, i2a = v2[0:8], i2[0:8]
    v2r, i2r = pltpu.roll(v2a, 4, axis=0), pltpu.roll(i2a, 4, axis=0)
    vals, flat, eidx = [], [], []

    def single(a, v2x, i2x, b0, nvalid):
        vals.append(jnp.where(sub < nvalid, v1[a:a + 1] + v2x, -jnp.inf))
        flat.append(a * PEER_TOPK + b0 + sub)
        eidx.append(i1[a:a + 1] * N_KEYS + i2x)

    def double(a, nvalid_a, nvalid_b):
        va = jnp.where(low, v1[a:a + 1], v1[a + 1:a + 2]) + jnp.where(low, v2a, v2r)
        ok = b_lo < jnp.where(low, nvalid_a, nvalid_b)
        vals.append(jnp.where(ok, va, -jnp.inf))
        flat.append(jnp.where(low, a * PEER_TOPK, (a + 1) * PEER_TOPK) + b_lo)
        eidx.append(jnp.where(low, i1[a:a + 1], i1[a + 1:a + 2]) * N_KEYS + jnp.where(low, i2a, i2r))

    single(0, v2a, i2a, 0, 8)
    single(0, v2[8:16], i2[8:16], 8, 8)
    single(1, v2a, i2a, 0, 8)
    single(2, v2a, i2a, 0, 5)
    single(3, v2a, i2a, 0, 4)
    double(4, 3, 2)
    double(6, 2, 2)
    vals.append(v1[8:16] + v2[0:1])
    flat.append((8 + sub) * PEER_TOPK)
    eidx.append(i1[8:16] * N_KEYS + i2[0:1])
    cat = lambda xs: jnp.concatenate(xs, axis=0)
    return cat(vals), cat(flat), cat(eidx)


def _topk_kernel(qp_ref, k1_ref, k2_ref, e_ref, g_ref):
    nt = (((1,), (1,)), ((), ()))
    rid = lax.broadcasted_iota(jnp.int32, (N_KEYS, qp_ref.shape[0]), 0)
    for hd in range(PEER_HEADS):
        c0 = hd * 2 * N_KEYS
        s1 = lax.dot_general(k1_ref[hd], qp_ref[:, c0:c0 + N_KEYS], nt, preferred_element_type=F32)
        s2 = lax.dot_general(k2_ref[hd], qp_ref[:, c0 + N_KEYS:c0 + 2 * N_KEYS], nt, preferred_element_type=F32)
        v1, i1 = _topk_rows(s1, PEER_TOPK, rid)
        v2, i2 = _topk_rows(s2, PEER_TOPK, rid)
        cand, flat, eidx = _pair_candidates(v1, i1, v2, i2)
        sv, ex = _topk_rows(cand, PEER_TOPK, flat, payload=eidx)
        p = jnp.exp(sv - sv[0:1, :])
        g_ref[hd * PEER_TOPK:(hd + 1) * PEER_TOPK, :] = p / jnp.sum(p, axis=0, keepdims=True)
        e_ref[hd * PEER_TOPK:(hd + 1) * PEER_TOPK, :] = ex


def _topk(qp, k1, k2, tb):
    n = qp.shape[0]
    nb = n // tb
    oblk = pl.BlockSpec((None, N_ACT, tb), lambda i: (i, 0, 0))
    return pl.pallas_call(
        _topk_kernel,
        grid=(nb,),
        in_specs=[pl.BlockSpec((tb, qp.shape[1]), lambda i: (i, 0)), _whole(), _whole()],
        out_specs=[oblk, oblk],
        out_shape=[jax.ShapeDtypeStruct((nb, N_ACT, tb), jnp.int32), jax.ShapeDtypeStruct((nb, N_ACT, tb), F32)],
        compiler_params=_cparams(("parallel",)),
        name="topk",
    )(qp, k1, k2)


def _pack_table(w):
    wb = w.astype(BF16)
    half = w.shape[1] // 2
    lo = lax.bitcast_convert_type(wb[:, :half], jnp.uint16).astype(jnp.uint32)
    hi = lax.bitcast_convert_type(wb[:, half:], jnp.uint16).astype(jnp.uint32)
    pk = lax.bitcast_convert_type(lo | (hi << 16), jnp.int32)
    return pk.reshape(w.shape[0] * ROWS_PER_EXPERT, LANES)


GATHER_BUFS = 8
SPLIT_ROWS = 8
NT_DIMS = (((1,), (1,)), ((), ()))


def _gather_rows(idx_ref, tab_ref, g_ref, t):
    row = idx_ref.at[t]
    for j in range(N_ACT):
        e4 = pl.multiple_of(row[j], ROWS_PER_EXPERT)
        g_ref[ROWS_PER_EXPERT * j:ROWS_PER_EXPERT * (j + 1), :] = tab_ref[pl.ds(e4, ROWS_PER_EXPERT), :]


def _token_pipeline(tb, idx_ref, tab_ref, compute, bufs, compute_first):
    grp = len(bufs) // 2
    sets = (bufs[:grp], bufs[grp:])
    for k in range(grp):
        _gather_rows(idx_ref, tab_ref, sets[0][k], k)

    def phase(src, dst, t0):
        for k in range(grp):
            nxt = jnp.minimum(t0 + grp + k, tb - 1)
            if compute_first:
                compute(t0 + k, src[k])
                _gather_rows(idx_ref, tab_ref, dst[k], nxt)
            else:
                _gather_rows(idx_ref, tab_ref, dst[k], nxt)
                compute(t0 + k, src[k])

    def body(it, c):
        t0 = it * grp
        for parity in range(2):
            @pl.when(it % 2 == parity)
            def _():
                phase(sets[parity], sets[1 - parity], t0)
        return c

    lax.fori_loop(0, tb // grp, body, 0)


def _expert_weights(g_ref, s):
    return pltpu.bitcast(g_ref[pl.ds(s, N_ACT, stride=ROWS_PER_EXPERT), :], BF16)


def _split_rows(ref, tb, rows):
    @pl.when(pl.program_id(0) == 0)
    def _():
        ref[...] = jnp.zeros(ref.shape, ref.dtype)

    for r, val in enumerate(rows):
        for c in range(ref.shape[0]):
            ref.at[c][pl.ds(r, tb, stride=SPLIT_ROWS), :] = val[:, c * LANES:(c + 1) * LANES]


def _merge_rows(ref, tb, r):
    return jnp.concatenate([ref.at[c][pl.ds(r, tb, stride=SPLIT_ROWS), :] for c in range(ref.shape[0])], axis=1)


def _load_tile(ref, t8):
    return jnp.concatenate([ref.at[c][pl.ds(t8, SPLIT_ROWS), :] for c in range(ref.shape[0])], axis=1)


def _store_tile(ref, t8, val):
    for c in range(ref.shape[0]):
        ref.at[c][pl.ds(t8, SPLIT_ROWS), :] = val[:, c * LANES:(c + 1) * LANES]


def _tile_scratch(tb, width):
    return pltpu.VMEM((width // LANES, tb * SPLIT_ROWS, LANES), F32)


def _hi_lo(x):
    hi = x.astype(BF16).astype(F32)
    return hi, x - hi


def _peer_u_body(idx_ref, h_ref, gate_ref, tab_ref, act_ref, *scratch, tb):
    bufs, (hl_ref, r_ref) = scratch[:GATHER_BUFS], scratch[GATHER_BUFS:]
    half = D_MODEL // 2
    hhi, hlo = _hi_lo(h_ref[...])
    _split_rows(hl_ref, tb, (hhi[:, :half], hlo[:, :half], hhi[:, half:], hlo[:, half:]))

    def compute(t, g_ref):
        t8 = pl.multiple_of(t * SPLIT_ROWS, SPLIT_ROWS)
        lhs = _load_tile(hl_ref, t8).astype(BF16)
        acc = jnp.zeros((SPLIT_ROWS, 2 * N_ACT), F32)
        for s in range(ROWS_PER_EXPERT):
            acc = acc + lax.dot_general(lhs[:, s * LANES:(s + 1) * LANES], _expert_weights(g_ref, s), NT_DIMS,
                                        preferred_element_type=F32)
        _store_tile(r_ref, t8, acc)

    _token_pipeline(tb, idx_ref, tab_ref, compute, bufs, compute_first=True)
    r = [_merge_rows(r_ref, tb, k) for k in range(4)]
    col = lax.broadcasted_iota(jnp.int32, (tb, 2 * N_ACT), 1)
    rc = jnp.where(col % 2 == 0, r[0] + r[1], r[2] + r[3])
    pair = jnp.where(lax.broadcasted_iota(jnp.int32, (2 * N_ACT, N_ACT), 0) // 2
                     == lax.broadcasted_iota(jnp.int32, (2 * N_ACT, N_ACT), 1), 1.0, 0.0).astype(F32)
    a = jnp.dot(rc, pair, preferred_element_type=F32, precision=lax.Precision.HIGHEST)
    act_ref[...] = _gelu(a) * jnp.transpose(gate_ref[...])


def _peer_u(idx, h2, gates_t, upk, tb):
    nb = idx.shape[0]
    return pl.pallas_call(
        functools.partial(_peer_u_body, tb=tb),
        grid=(nb,),
        in_specs=[pl.BlockSpec((None, tb, N_ACT), lambda i: (i, 0, 0), memory_space=pltpu.SMEM),
                  pl.BlockSpec((tb, D_MODEL), lambda i: (i, 0)),
                  pl.BlockSpec((None, N_ACT, tb), lambda i: (i, 0, 0)), _whole()],
        out_specs=pl.BlockSpec((None, tb, N_ACT), lambda i: (i, 0, 0)),
        out_shape=jax.ShapeDtypeStruct((nb, tb, N_ACT), F32),
        scratch_shapes=[pltpu.VMEM((N_ACT * ROWS_PER_EXPERT, LANES), jnp.int32)] * GATHER_BUFS
        + [_tile_scratch(tb, D_MODEL // 2), _tile_scratch(tb, 2 * N_ACT)],
        compiler_params=_cparams(("arbitrary",)),
        name="peer_u",
    )(idx, h2, gates_t, upk)


def _peer_v_body(idx_ref, act_ref, x1_ref, tab_ref, out_ref, *scratch, tb):
    bufs, (l_ref, o_ref) = scratch[:GATHER_BUFS], scratch[GATHER_BUFS:]
    ahi, alo = _hi_lo(act_ref[...])
    ahi, alo = ahi.astype(BF16), alo.astype(BF16)
    j2 = 2 * lax.broadcasted_iota(jnp.int32, (N_ACT, 2 * N_ACT), 0)
    col = lax.broadcasted_iota(jnp.int32, (N_ACT, 2 * N_ACT), 1)
    spread = [jnp.where(col == j2 + b, 1.0, 0.0).astype(BF16) for b in range(2)]
    _split_rows(l_ref, tb, [jnp.dot(a, e, preferred_element_type=F32) for e in spread for a in (ahi, alo)])

    def compute(t, g_ref):
        t8 = pl.multiple_of(t * SPLIT_ROWS, SPLIT_ROWS)
        lhs = _load_tile(l_ref, t8).astype(BF16)
        outs = [jnp.dot(lhs, _expert_weights(g_ref, s), preferred_element_type=F32) for s in range(ROWS_PER_EXPERT)]
        _store_tile(o_ref, t8, jnp.concatenate(outs, axis=1))

    _token_pipeline(tb, idx_ref, tab_ref, compute, bufs, compute_first=False)
    o = [_merge_rows(o_ref, tb, k) for k in range(4)]
    out_ref[...] = x1_ref[...] + jnp.concatenate([o[0] + o[1], o[2] + o[3]], axis=1)


def _peer_v(idx, act, x1, vpk, tb):
    nb = idx.shape[0]
    return pl.pallas_call(
        functools.partial(_peer_v_body, tb=tb),
        grid=(nb,),
        in_specs=[pl.BlockSpec((None, tb, N_ACT), lambda i: (i, 0, 0), memory_space=pltpu.SMEM),
                  pl.BlockSpec((None, tb, N_ACT), lambda i: (i, 0, 0)),
                  pl.BlockSpec((tb, D_MODEL), lambda i: (i, 0)), _whole()],
        out_specs=pl.BlockSpec((tb, D_MODEL), lambda i: (i, 0)),
        out_shape=jax.ShapeDtypeStruct(x1.shape, F32),
        scratch_shapes=[pltpu.VMEM((N_ACT * ROWS_PER_EXPERT, LANES), jnp.int32)] * GATHER_BUFS
        + [_tile_scratch(tb, 2 * N_ACT), _tile_scratch(tb, D_MODEL // 2)],
        compiler_params=_cparams(("arbitrary",)),
        name="peer_v",
    )(idx, act, x1, vpk)


def _tile(n, pref):
    t = min(pref, n)
    while n % t:
        t //= 2
    return t


def kernel(x, mem, norm1_g, w_in, b_gate, b_forget, fox_q_g, fox_k_g, swa_q_g, swa_k_g, swa_sinks, mem_norm_g, w_mem_kv, mem_q_g, mem_k_g, w_fox_o, w_swa_o, w_mem_o, w_out, norm2_g, w_peer_q, peer_keys1, peer_keys2, peer_u, peer_v):
    b, s, d = x.shape
    n = b * s
    assert d == D_MODEL and s % 256 == 0 and mem.shape[1] == N_MEM
    depth = norm1_g.shape[0]
    xf = x.reshape(n, d)
    memf = mem.reshape(b * N_MEM, d)
    tb = BLOCK
    for l in range(depth):
        w = w_in[l]
        o_ff, o_sq, o_sk, o_sv, o_mq, o_gl = 1536, 1544, 2056, 2184, 2312, 2824
        w_att = jnp.concatenate(
            [w[:, 0:o_ff], w[:, o_sq:o_gl], w[:, o_ff:o_sq], jnp.zeros((d, LANES - FOX_HEADS), F32)],
            axis=1).astype(BF16)
        w_gate = w[:, o_gl:].astype(BF16)
        tile2 = lambda g: jnp.concatenate([g, g])
        gains = jnp.stack([tile2(fox_q_g[l]) * HEAD_D ** -0.5, tile2(fox_k_g[l]),
                           tile2(swa_q_g[l]) * HEAD_D ** -0.5, tile2(swa_k_g[l]),
                           mem_q_g[l] * MEM_HD ** -0.5] + [jnp.zeros((LANES,), F32)] * 3)
        bf = jnp.concatenate([b_forget[l], jnp.zeros((LANES - FOX_HEADS,), F32)]).reshape(1, LANES)
        g1 = norm1_g[l].reshape(1, d)

        fq, fk, fv, sq, sk, sv, mq, lf = _inproj(xf, g1, w_att, gains, bf, _tile(n, 512))
        ccol, crow = _cumsum(lf.reshape(b, s, LANES))
        o_fox = _fox(fq, fk, fv, ccol.reshape(n, LANES), crow.reshape(b * FOX_HEADS, 1, s), b, s, 256)
        o_swa = _swa(swa_sinks[l], sq, sk, sv, b, s)
        mk, mv = _memkv(memf, mem_norm_g[l].reshape(1, d), w_mem_kv[l].astype(BF16), mem_k_g[l].reshape(1, LANES))
        o_mem = _mem(mq, mk, mv, b, s, 256)

        x1, h2, qp = _merge(xf, o_fox, o_swa, o_mem, g1, w_gate, b_gate[l].reshape(1, 3 * d),
                            w_fox_o[l].astype(BF16), w_swa_o[l].astype(BF16), w_mem_o[l].astype(BF16),
                            w_out[l].astype(BF16), norm2_g[l].reshape(1, d), w_peer_q[l].astype(BF16), _tile(n, 256))

        e_t, gates_t = _topk(qp, peer_keys1[l].astype(BF16), peer_keys2[l].astype(BF16), tb)
        idx = jnp.swapaxes(e_t, 1, 2) * ROWS_PER_EXPERT
        act_t = _peer_u(idx, h2, gates_t, _pack_table(peer_u[l]), tb)
        xf = _peer_v(idx, act_t, x1, _pack_table(peer_v[l]), tb)
    return xf.reshape(b, s, d)
```

```python
import functools

import jax
import jax.numpy as jnp
from jax import lax
from jax.experimental import pallas as pl
from jax.experimental.pallas import tpu as pltpu

F32 = jnp.float32
BF16 = jnp.bfloat16

D_MODEL = 1024
BLOCK = 128
FOX_HEADS = 8
HEAD_D = 64
SWA_HEADS = 8
SWA_KV_HEADS = 2
MEM_HEADS = 4
MEM_HD = 128
N_MEM = 256
PEER_HEADS = 8
N_KEYS = 128
N_EXPERTS = N_KEYS * N_KEYS
PEER_TOPK = 16
N_ACT = PEER_HEADS * PEER_TOPK
EPS = 1e-6
NEG_INF = -1e30

LANES = 128
ROWS_PER_EXPERT = 4
VMEM_LIMIT = 48 * 1024 * 1024

C_FQ, C_FK, C_FV, C_SQ, C_SK, C_SV, C_MQ, C_FF = 0, 512, 1024, 1536, 2048, 2176, 2304, 2816
ATT_COLS = 2944


def _cparams(sem):
    return pltpu.CompilerParams(dimension_semantics=sem, vmem_limit_bytes=VMEM_LIMIT)


def _whole():
    return pl.BlockSpec(memory_space=pltpu.VMEM)


def _rms(xf, g):
    return xf * lax.rsqrt(jnp.mean(xf * xf, axis=-1, keepdims=True) + EPS) * g


def _seg_matrix(seg):
    r = lax.broadcasted_iota(jnp.int32, (LANES, LANES), 0) // seg
    c = lax.broadcasted_iota(jnp.int32, (LANES, LANES), 1) // seg
    return jnp.where(r == c, 1.0 / seg, 0.0).astype(BF16)


def _seg_norm(z, seg_mat, gain):
    sq = z * z
    hi = sq.astype(BF16)
    lo = (sq - hi.astype(F32)).astype(BF16)
    ms = jnp.dot(hi, seg_mat, preferred_element_type=F32) + jnp.dot(lo, seg_mat, preferred_element_type=F32)
    return z * lax.rsqrt(ms + EPS) * gain


def _log_sigmoid(x):
    return jnp.minimum(x, 0.0) - jnp.log1p(jnp.exp(-jnp.abs(x)))


def _sigmoid(x):
    return 1.0 / (1.0 + jnp.exp(-x))


def _gelu(x):
    return 0.5 * x * (1.0 + lax.erf(x * 0.7071067811865476))


def _inproj_kernel(x_ref, g1_ref, w_ref, gains_ref, bf_ref,
                   fq_ref, fk_ref, fv_ref, sq_ref, sk_ref, sv_ref, mq_ref, lf_ref):
    h = _rms(x_ref[...], g1_ref[...]).astype(BF16)
    seg64 = _seg_matrix(HEAD_D)
    seg128 = _seg_matrix(MEM_HD)

    def proj(c0, width):
        return jnp.dot(h, w_ref[:, c0:c0 + width], preferred_element_type=F32)

    def normed(c0, nblk, seg_mat, gain_row, out_ref):
        gain = gains_ref[gain_row:gain_row + 1, :]
        for b in range(nblk):
            z = proj(c0 + b * LANES, LANES)
            out_ref[:, b * LANES:(b + 1) * LANES] = _seg_norm(z, seg_mat, gain).astype(BF16)

    normed(C_FQ, 4, seg64, 0, fq_ref)
    normed(C_FK, 4, seg64, 1, fk_ref)
    fv_ref[...] = proj(C_FV, 512).astype(BF16)
    normed(C_SQ, 4, seg64, 2, sq_ref)
    normed(C_SK, 1, seg64, 3, sk_ref)
    sv_ref[...] = proj(C_SV, 128).astype(BF16)
    normed(C_MQ, 4, seg128, 4, mq_ref)
    lf_ref[...] = _log_sigmoid(proj(C_FF, 128) + bf_ref[...])


def _inproj(xf, g1, w_att, gains, bf, tm):
    n = xf.shape[0]
    row = lambda w: pl.BlockSpec((tm, w), lambda i: (i, 0))
    out_shape = [jax.ShapeDtypeStruct((n, w), BF16) for w in (512, 512, 512, 512, 128, 128, 512)]
    out_shape.append(jax.ShapeDtypeStruct((n, 128), F32))
    return pl.pallas_call(
        _inproj_kernel,
        grid=(n // tm,),
        in_specs=[row(D_MODEL), _whole(), _whole(), _whole(), _whole()],
        out_specs=[row(512), row(512), row(512), row(512), row(128), row(128), row(512), row(128)],
        out_shape=out_shape,
        compiler_params=_cparams(("parallel",)),
        name="inproj",
    )(xf, g1, w_att, gains, bf)


def _cumsum_kernel(lf_ref, ccol_ref, crow_ref, *, blk):
    s = lf_ref.shape[0]
    r = lax.broadcasted_iota(jnp.int32, (blk, blk), 0)
    c = lax.broadcasted_iota(jnp.int32, (blk, blk), 1)
    tri = jnp.where(c <= r, 1.0, 0.0).astype(F32)
    carry = jnp.zeros((1, LANES), F32)
    for b in range(s // blk):
        x = lf_ref[b * blk:(b + 1) * blk, :]
        cs = jnp.dot(tri, x, preferred_element_type=F32, precision=lax.Precision.HIGHEST) + carry
        ccol_ref[b * blk:(b + 1) * blk, :] = cs
        carry = cs[blk - 1:blk, :]
    ct = jnp.transpose(ccol_ref[...])
    crow_ref[...] = ct[0:FOX_HEADS, :]


def _cumsum(lf3):
    b, s, _ = lf3.shape
    return pl.pallas_call(
        functools.partial(_cumsum_kernel, blk=256),
        grid=(b,),
        in_specs=[pl.BlockSpec((None, s, LANES), lambda i: (i, 0, 0))],
        out_specs=[pl.BlockSpec((None, s, LANES), lambda i: (i, 0, 0)),
                   pl.BlockSpec((None, FOX_HEADS, s), lambda i: (i, 0, 0))],
        out_shape=[jax.ShapeDtypeStruct((b, s, LANES), F32), jax.ShapeDtypeStruct((b, FOX_HEADS, s), F32)],
        compiler_params=_cparams(("parallel",)),
        name="cumsum",
    )(lf3)


def _fox_kernel(q_ref, k_ref, v_ref, ccol_ref, crow0_ref, crow1_ref, o_ref, m_scr, l_scr, acc_scr, *, tq):
    hp = pl.program_id(1)
    qi = pl.program_id(2)
    lane = lax.broadcasted_iota(jnp.int32, (tq, LANES), 1)
    q = q_ref[...]
    zero = jnp.zeros_like(q)
    qh = (jnp.where(lane < HEAD_D, q, zero), jnp.where(lane >= HEAD_D, q, zero))
    cc = ccol_ref[...]
    ci = tuple(jnp.sum(jnp.where(lane == 2 * hp + a, cc, 0.0), axis=1, keepdims=True) for a in range(2))
    crow = (crow0_ref, crow1_ref)
    m_scr[...] = jnp.full(m_scr.shape, NEG_INF, F32)
    l_scr[...] = jnp.zeros(l_scr.shape, F32)
    acc_scr[...] = jnp.zeros(acc_scr.shape, F32)
    rows = lax.broadcasted_iota(jnp.int32, (tq, tq), 0)
    cols = lax.broadcasted_iota(jnp.int32, (tq, tq), 1)

    def step(j, masked):
        ks = pl.multiple_of(j * tq, tq)
        kb = k_ref[pl.ds(ks, tq), :]
        vb = v_ref[pl.ds(ks, tq), :]
        for a in range(2):
            s = lax.dot_general(qh[a], kb, (((1,), (1,)), ((), ())), preferred_element_type=F32)
            s = s + (ci[a] - crow[a][:, pl.ds(ks, tq)])
            if masked:
                s = jnp.where(cols <= rows, s, NEG_INF)
            m_prev = m_scr[a]
            m_next = jnp.maximum(m_prev, jnp.max(s, axis=1, keepdims=True))
            p = jnp.exp(s - jnp.concatenate([m_next] * (tq // LANES), axis=1))
            alpha = jnp.exp(m_prev - m_next)
            l_scr[a] = alpha * l_scr[a] + jnp.sum(p, axis=1, keepdims=True)
            acc_scr[a] = alpha * acc_scr[a] + jnp.dot(p.astype(BF16), vb, preferred_element_type=F32)
            m_scr[a] = m_next

    def body(j, c):
        step(j, False)
        return c

    lax.fori_loop(0, qi, body, 0)
    step(qi, True)
    o0 = acc_scr[0] / l_scr[0]
    o1 = acc_scr[1] / l_scr[1]
    o_ref[...] = jnp.where(lane < HEAD_D, o0, o1).astype(BF16)


def _fox(fq, fk, fv, ccol, crow, b, s, tq):
    nq = s // tq
    qspec = pl.BlockSpec((tq, LANES), lambda bi, hp, qi: (bi * nq + qi, hp))
    kvspec = pl.BlockSpec((s, LANES), lambda bi, hp, qi: (bi, hp))
    return pl.pallas_call(
        functools.partial(_fox_kernel, tq=tq),
        grid=(b, FOX_HEADS // 2, nq),
        in_specs=[qspec, kvspec, kvspec,
                  pl.BlockSpec((tq, LANES), lambda bi, hp, qi: (bi * nq + qi, 0)),
                  pl.BlockSpec((None, 1, s), lambda bi, hp, qi: (bi * FOX_HEADS + 2 * hp, 0, 0)),
                  pl.BlockSpec((None, 1, s), lambda bi, hp, qi: (bi * FOX_HEADS + 2 * hp + 1, 0, 0))],
        out_specs=qspec,
        out_shape=jax.ShapeDtypeStruct(fq.shape, BF16),
        scratch_shapes=[pltpu.VMEM((2, tq, LANES), F32), pltpu.VMEM((2, tq, LANES), F32),
                        pltpu.VMEM((2, tq, LANES), F32)],
        compiler_params=_cparams(("parallel", "parallel", "arbitrary")),
        name="fox",
    )(fq, fk, fv, ccol, crow, crow)


def _swa_kernel(sink_ref, q_ref, kp_ref, kc_ref, vp_ref, vc_ref, o_ref):
    blk = pl.program_id(1)
    lane = lax.broadcasted_iota(jnp.int32, (BLOCK, LANES), 1)
    kb = jnp.concatenate([kp_ref[...], kc_ref[...]], axis=0)
    vb = jnp.concatenate([vp_ref[...], vc_ref[...]], axis=0)
    qi = lax.broadcasted_iota(jnp.int32, (BLOCK, 2 * BLOCK), 0)
    kj = lax.broadcasted_iota(jnp.int32, (BLOCK, 2 * BLOCK), 1)
    dist = qi + BLOCK - kj
    ok = (dist >= 0) & (dist < BLOCK) & ((blk > 0) | (kj >= BLOCK))
    distf = dist.astype(F32)
    rep = SWA_HEADS // SWA_KV_HEADS
    for pair in range(SWA_HEADS // 2):
        q128 = q_ref[:, pair * LANES:(pair + 1) * LANES]
        outs = []
        for half in range(2):
            h = 2 * pair + half
            grp = h // rep
            qa = q128 if half == grp else pltpu.roll(q128, HEAD_D, axis=1)
            in_grp = (lane >= grp * HEAD_D) & (lane < (grp + 1) * HEAD_D)
            qa = jnp.where(in_grp, qa, jnp.zeros_like(qa))
            s = lax.dot_general(qa, kb, (((1,), (1,)), ((), ())), preferred_element_type=F32)
            s = s - (2.0 ** -(h + 1)) * distf
            s = jnp.where(ok, s, NEG_INF)
            sink = sink_ref[h]
            m = jnp.maximum(jnp.max(s, axis=1, keepdims=True), sink)
            p = jnp.exp(s - m)
            den = jnp.sum(p, axis=1, keepdims=True) + jnp.exp(sink - m)
            o = jnp.dot(p.astype(BF16), vb, preferred_element_type=F32) / den
            outs.append(o if half == grp else pltpu.roll(o, HEAD_D, axis=1))
        o_ref[:, pair * LANES:(pair + 1) * LANES] = jnp.where(lane < HEAD_D, outs[0], outs[1]).astype(BF16)


def _swa(sinks, sq, sk, sv, b, s):
    nb = s // BLOCK
    cur = lambda bi, i: (bi * nb + i, 0)
    prev = lambda bi, i: (bi * nb + jnp.maximum(i - 1, 0), 0)
    return pl.pallas_call(
        _swa_kernel,
        grid=(b, nb),
        in_specs=[pl.BlockSpec(memory_space=pltpu.SMEM),
                  pl.BlockSpec((BLOCK, 512), cur),
                  pl.BlockSpec((BLOCK, LANES), prev), pl.BlockSpec((BLOCK, LANES), cur),
                  pl.BlockSpec((BLOCK, LANES), prev), pl.BlockSpec((BLOCK, LANES), cur)],
        out_specs=pl.BlockSpec((BLOCK, 512), cur),
        out_shape=jax.ShapeDtypeStruct(sq.shape, BF16),
        compiler_params=_cparams(("parallel", "parallel")),
        name="swa",
    )(sinks, sq, sk, sk, sv, sv)


def _memkv_kernel(mem_ref, g_ref, w_ref, gk_ref, mk_ref, mv_ref):
    mn = _rms(mem_ref[...], g_ref[...]).astype(BF16)
    seg128 = _seg_matrix(MEM_HD)
    width = MEM_HEADS * MEM_HD
    for hd in range(MEM_HEADS):
        z = jnp.dot(mn, w_ref[:, hd * LANES:(hd + 1) * LANES], preferred_element_type=F32)
        mk_ref[:, hd * LANES:(hd + 1) * LANES] = _seg_norm(z, seg128, gk_ref[...]).astype(BF16)
    mv_ref[...] = jnp.dot(mn, w_ref[:, width:2 * width], preferred_element_type=F32).astype(BF16)


def _memkv(memf, g, w, gk):
    n = memf.shape[0]
    width = MEM_HEADS * MEM_HD
    tm = N_MEM
    return pl.pallas_call(
        _memkv_kernel,
        grid=(n // tm,),
        in_specs=[pl.BlockSpec((tm, D_MODEL), lambda i: (i, 0)), _whole(), _whole(), _whole()],
        out_specs=[pl.BlockSpec((tm, width), lambda i: (i, 0)), pl.BlockSpec((tm, width), lambda i: (i, 0))],
        out_shape=[jax.ShapeDtypeStruct((n, width), BF16), jax.ShapeDtypeStruct((n, width), BF16)],
        compiler_params=_cparams(("parallel",)),
        name="memkv",
    )(memf, g, w, gk)


def _mem_kernel(q_ref, k_ref, v_ref, o_ref):
    for hd in range(MEM_HEADS):
        sl = slice(hd * LANES, (hd + 1) * LANES)
        s = lax.dot_general(q_ref[:, sl], k_ref[:, sl], (((1,), (1,)), ((), ())), preferred_element_type=F32)
        m = jnp.max(s, axis=1, keepdims=True)
        p = jnp.exp(s - m)
        den = jnp.sum(p, axis=1, keepdims=True)
        o_ref[:, sl] = (jnp.dot(p.astype(BF16), v_ref[:, sl], preferred_element_type=F32) / den).astype(BF16)


def _mem(mq, mk, mv, b, s, tq):
    nq = s // tq
    width = MEM_HEADS * MEM_HD
    qspec = pl.BlockSpec((tq, width), lambda bi, qi: (bi * nq + qi, 0))
    kvspec = pl.BlockSpec((N_MEM, width), lambda bi, qi: (bi, 0))
    return pl.pallas_call(
        _mem_kernel,
        grid=(b, nq),
        in_specs=[qspec, kvspec, kvspec],
        out_specs=qspec,
        out_shape=jax.ShapeDtypeStruct(mq.shape, BF16),
        compiler_params=_cparams(("parallel", "parallel")),
        name="mem",
    )(mq, mk, mv)


def _merge_kernel(x_ref, of_ref, os_ref, om_ref, g1_ref, wg_ref, bg_ref, wfo_ref, wso_ref, wmo_ref, wout_ref,
                  g2_ref, wpq_ref, x1_ref, h2_ref, qp_ref):
    x = x_ref[...]
    h = _rms(x, g1_ref[...]).astype(BF16)
    merged = jnp.zeros(x.shape, F32)
    for br, (o_ref, w_ref) in enumerate(((of_ref, wfo_ref), (os_ref, wso_ref), (om_ref, wmo_ref))):
        sl = slice(br * D_MODEL, (br + 1) * D_MODEL)
        gate = _sigmoid(jnp.dot(h, wg_ref[:, sl], preferred_element_type=F32) + bg_ref[:, sl])
        merged = merged + gate * jnp.dot(o_ref[...], w_ref[...], preferred_element_type=F32)
    x1 = x + jnp.dot(merged.astype(BF16), wout_ref[...], preferred_element_type=F32)
    x1_ref[...] = x1
    h2 = _rms(x1, g2_ref[...])
    h2_ref[...] = h2
    qp_ref[...] = jnp.dot(h2.astype(BF16), wpq_ref[...], preferred_element_type=F32).astype(BF16)


def _merge(xf, of, os_, om, g1, wg, bg, wfo, wso, wmo, wout, g2, wpq, tm):
    n = xf.shape[0]
    row = lambda w: pl.BlockSpec((tm, w), lambda i: (i, 0))
    qw = wpq.shape[1]
    return pl.pallas_call(
        _merge_kernel,
        grid=(n // tm,),
        in_specs=[row(D_MODEL), row(512), row(512), row(512)] + [_whole()] * 9,
        out_specs=[row(D_MODEL), row(D_MODEL), row(qw)],
        out_shape=[jax.ShapeDtypeStruct((n, D_MODEL), F32), jax.ShapeDtypeStruct((n, D_MODEL), F32),
                   jax.ShapeDtypeStruct((n, qw), BF16)],
        compiler_params=_cparams(("parallel",)),
        name="merge",
    )(xf, of, os_, om, g1, wg, bg, wfo, wso, wmo, wout, g2, wpq)


def _topk_rows(s, k, order, payload=None):
    big = jnp.int32(2 ** 30)
    vals, sels = [], []
    for _ in range(k):
        m = jnp.max(s, axis=0, keepdims=True)
        sel = jnp.min(jnp.where(s == m, order, big), axis=0, keepdims=True)
        hit = order == sel
        vals.append(m)
        if payload is None:
            sels.append(sel)
        else:
            sels.append(jnp.max(jnp.where(hit, payload, -1), axis=0, keepdims=True))
        s = jnp.where(hit, -jnp.inf, s)
    return jnp.concatenate(vals, axis=0), jnp.concatenate(sels, axis=0)


def _pair_candidates(v1, i1, v2, i2):
    t = v1.shape[1]
    sub = lax.broadcasted_iota(jnp.int32, (8, t), 0)
    low = sub < 4
    b_lo = jnp.where(low, sub, sub - 4)
    v2a, i2a = v2[0:8], i2[0:8]
    v2r, i2r = pltpu.roll(v2a, 4, axis=0), pltpu.roll(i2a, 4, axis=0)
    vals, flat, eidx = [], [], []

    def single(a, v2x, i2x, b0, nvalid):
        vals.append(jnp.where(sub < nvalid, v1[a:a + 1] + v2x, -jnp.inf))
        flat.append(a * PEER_TOPK + b0 + sub)
        eidx.append(i1[a:a + 1] * N_KEYS + i2x)

    def double(a, nvalid_a, nvalid_b):
        va = jnp.where(low, v1[a:a + 1], v1[a + 1:a + 2]) + jnp.where(low, v2a, v2r)
        ok = b_lo < jnp.where(low, nvalid_a, nvalid_b)
        vals.append(jnp.where(ok, va, -jnp.inf))
        flat.append(jnp.where(low, a * PEER_TOPK, (a + 1) * PEER_TOPK) + b_lo)
        eidx.append(jnp.where(low, i1[a:a + 1], i1[a + 1:a + 2]) * N_KEYS + jnp.where(low, i2a, i2r))

    single(0, v2a, i2a, 0, 8)
    single(0, v2[8:16], i2[8:16], 8, 8)
    single(1, v2a, i2a, 0, 8)
    single(2, v2a, i2a, 0, 5)
    single(3, v2a, i2a, 0, 4)
    double(4, 3, 2)
    double(6, 2, 2)
    vals.append(v1[8:16] + v2[0:1])
    flat.append((8 + sub) * PEER_TOPK)
    eidx.append(i1[8:16] * N_KEYS + i2[0:1])
    cat = lambda xs: jnp.concatenate(xs, axis=0)
    return cat(vals), cat(flat), cat(eidx)


def _topk_kernel(qp_ref, k1_ref, k2_ref, e_ref, g_ref):
    nt = (((1,), (1,)), ((), ()))
    rid = lax.broadcasted_iota(jnp.int32, (N_KEYS, qp_ref.shape[0]), 0)
    for hd in range(PEER_HEADS):
        c0 = hd * 2 * N_KEYS
        s1 = lax.dot_general(k1_ref[hd], qp_ref[:, c0:c0 + N_KEYS], nt, preferred_element_type=F32)
        s2 = lax.dot_general(k2_ref[hd], qp_ref[:, c0 + N_KEYS:c0 + 2 * N_KEYS], nt, preferred_element_type=F32)
        v1, i1 = _topk_rows(s1, PEER_TOPK, rid)
        v2, i2 = _topk_rows(s2, PEER_TOPK, rid)
        cand, flat, eidx = _pair_candidates(v1, i1, v2, i2)
        sv, ex = _topk_rows(cand, PEER_TOPK, flat, payload=eidx)
        p = jnp.exp(sv - sv[0:1, :])
        g_ref[hd * PEER_TOPK:(hd + 1) * PEER_TOPK, :] = p / jnp.sum(p, axis=0, keepdims=True)
        e_ref[hd * PEER_TOPK:(hd + 1) * PEER_TOPK, :] = ex


def _topk(qp, k1, k2, tb):
    n = qp.shape[0]
    nb = n // tb
    oblk = pl.BlockSpec((None, N_ACT, tb), lambda i: (i, 0, 0))
    return pl.pallas_call(
        _topk_kernel,
        grid=(nb,),
        in_specs=[pl.BlockSpec((tb, qp.shape[1]), lambda i: (i, 0)), _whole(), _whole()],
        out_specs=[oblk, oblk],
        out_shape=[jax.ShapeDtypeStruct((nb, N_ACT, tb), jnp.int32), jax.ShapeDtypeStruct((nb, N_ACT, tb), F32)],
        compiler_params=_cparams(("parallel",)),
        name="topk",
    )(qp, k1, k2)


def _pack_table(w):
    wb = w.astype(BF16)
    half = w.shape[1] // 2
    lo = lax.bitcast_convert_type(wb[:, :half], jnp.uint16).astype(jnp.uint32)
    hi = lax.bitcast_convert_type(wb[:, half:], jnp.uint16).astype(jnp.uint32)
    pk = lax.bitcast_convert_type(lo | (hi << 16), jnp.int32)
    return pk.reshape(w.shape[0] * ROWS_PER_EXPERT, LANES)


GATHER_BUFS = 8
SPLIT_ROWS = 8
NT_DIMS = (((1,), (1,)), ((), ()))


def _group_index_layout(e_tok, grp):
    nb, tb, n_act = e_tok.shape
    seg = n_act // grp
    x = e_tok.reshape(nb, tb // grp, grp, grp, seg)
    return jnp.swapaxes(x, 2, 3).reshape(nb, tb, n_act)


GATHER_CHUNKS = 4
PEER_U_SPAN = 1.0
PEER_V_SPAN = 0.6


def _gather_rows(idx_ref, tab_ref, g_ref, g0, k, grp, chunk=None):
    seg = N_ACT // grp
    rows = [idx_ref.at[g0 + kk] for kk in range(grp)]
    per = seg // GATHER_CHUNKS
    jos = range(seg) if chunk is None else range(chunk * per, (chunk + 1) * per)
    for jo in jos:
        for kk in range(grp):
            j = kk * seg + jo
            e4 = pl.multiple_of(rows[kk][k * seg + jo], ROWS_PER_EXPERT)
            g_ref[ROWS_PER_EXPERT * j:ROWS_PER_EXPERT * (j + 1), :] = tab_ref[pl.ds(e4, ROWS_PER_EXPERT), :]


def _token_pipeline(tb, idx_ref, tab_ref, compute, bufs, compute_span):
    grp = len(bufs) // 2
    sets = (bufs[:grp], bufs[grp:])
    for k in range(grp):
        _gather_rows(idx_ref, tab_ref, sets[0][k], 0, k, grp)

    def phase(src, dst, t0):
        nxt = jnp.minimum(t0 + grp, tb - grp)
        steps = [step for k in range(grp) for step in [compute(t0 + k, src[k])] * ROWS_PER_EXPERT]
        n_gather = grp * GATHER_CHUNKS
        done = 0
        for i in range(n_gather):
            while done < len(steps) and int(done * compute_span * n_gather / len(steps)) <= i:
                next(steps[done], None)
                done += 1
            _gather_rows(idx_ref, tab_ref, dst[i // GATHER_CHUNKS], nxt, i // GATHER_CHUNKS, grp, i % GATHER_CHUNKS)
        for gen in steps[done:]:
            next(gen, None)

    def body(it, c):
        t0 = it * grp
        for parity in range(2):
            @pl.when(it % 2 == parity)
            def _():
                phase(sets[parity], sets[1 - parity], t0)
        return c

    lax.fori_loop(0, tb // grp, body, 0)


def _expert_weights(g_ref, s):
    return pltpu.bitcast(g_ref[pl.ds(s, N_ACT, stride=ROWS_PER_EXPERT), :], BF16)


def _split_rows(ref, tb, rows):
    @pl.when(pl.program_id(0) == 0)
    def _():
        ref[...] = jnp.zeros(ref.shape, ref.dtype)

    for r, val in enumerate(rows):
        for c in range(ref.shape[0]):
            ref.at[c][pl.ds(r, tb, stride=SPLIT_ROWS), :] = val[:, c * LANES:(c + 1) * LANES]


def _merge_rows(ref, tb, r):
    return jnp.concatenate([ref.at[c][pl.ds(r, tb, stride=SPLIT_ROWS), :] for c in range(ref.shape[0])], axis=1)


def _load_tile(ref, t8):
    return jnp.concatenate([ref.at[c][pl.ds(t8, SPLIT_ROWS), :] for c in range(ref.shape[0])], axis=1)


def _store_tile(ref, t8, val):
    for c in range(ref.shape[0]):
        ref.at[c][pl.ds(t8, SPLIT_ROWS), :] = val[:, c * LANES:(c + 1) * LANES]


def _tile_scratch(tb, width):
    return pltpu.VMEM((width // LANES, tb * SPLIT_ROWS, LANES), F32)


def _hi_lo(x):
    hi = x.astype(BF16).astype(F32)
    return hi, x - hi


def _peer_u_body(idx_ref, h_ref, gate_ref, tab_ref, act_ref, *scratch, tb):
    bufs, (hl_ref, r_ref) = scratch[:GATHER_BUFS], scratch[GATHER_BUFS:]
    half = D_MODEL // 2
    hhi, hlo = _hi_lo(h_ref[...])
    _split_rows(hl_ref, tb, (hhi[:, :half], hlo[:, :half], hhi[:, half:], hlo[:, half:]))

    def compute(t, g_ref):
        t8 = pl.multiple_of(t * SPLIT_ROWS, SPLIT_ROWS)
        lhs = _load_tile(hl_ref, t8).astype(BF16)
        acc = jnp.zeros((SPLIT_ROWS, 2 * N_ACT), F32)
        for s in range(ROWS_PER_EXPERT):
            acc = acc + lax.dot_general(lhs[:, s * LANES:(s + 1) * LANES], _expert_weights(g_ref, s), NT_DIMS,
                                        preferred_element_type=F32)
            if s == ROWS_PER_EXPERT - 1:
                _store_tile(r_ref, t8, acc)
            yield

    _token_pipeline(tb, idx_ref, tab_ref, compute, bufs, compute_span=PEER_U_SPAN)
    r = [_merge_rows(r_ref, tb, k) for k in range(4)]
    col = lax.broadcasted_iota(jnp.int32, (tb, 2 * N_ACT), 1)
    rc = jnp.where(col % 2 == 0, r[0] + r[1], r[2] + r[3])
    pair = jnp.where(lax.broadcasted_iota(jnp.int32, (2 * N_ACT, N_ACT), 0) // 2
                     == lax.broadcasted_iota(jnp.int32, (2 * N_ACT, N_ACT), 1), 1.0, 0.0).astype(F32)
    a = jnp.dot(rc, pair, preferred_element_type=F32, precision=lax.Precision.HIGHEST)
    act_ref[...] = _gelu(a) * jnp.transpose(gate_ref[...])


def _peer_u(idx, h2, gates_t, upk, tb):
    nb = idx.shape[0]
    return pl.pallas_call(
        functools.partial(_peer_u_body, tb=tb),
        grid=(nb,),
        in_specs=[pl.BlockSpec((None, tb, N_ACT), lambda i: (i, 0, 0), memory_space=pltpu.SMEM),
                  pl.BlockSpec((tb, D_MODEL), lambda i: (i, 0)),
                  pl.BlockSpec((None, N_ACT, tb), lambda i: (i, 0, 0)), _whole()],
        out_specs=pl.BlockSpec((None, tb, N_ACT), lambda i: (i, 0, 0)),
        out_shape=jax.ShapeDtypeStruct((nb, tb, N_ACT), F32),
        scratch_shapes=[pltpu.VMEM((N_ACT * ROWS_PER_EXPERT, LANES), jnp.int32)] * GATHER_BUFS
        + [_tile_scratch(tb, D_MODEL // 2), _tile_scratch(tb, 2 * N_ACT)],
        compiler_params=_cparams(("arbitrary",)),
        name="peer_u",
    )(idx, h2, gates_t, upk)


def _peer_v_body(idx_ref, act_ref, x1_ref, tab_ref, out_ref, *scratch, tb):
    bufs, (l_ref, o_ref) = scratch[:GATHER_BUFS], scratch[GATHER_BUFS:]
    ahi, alo = _hi_lo(act_ref[...])
    ahi, alo = ahi.astype(BF16), alo.astype(BF16)
    j2 = 2 * lax.broadcasted_iota(jnp.int32, (N_ACT, 2 * N_ACT), 0)
    col = lax.broadcasted_iota(jnp.int32, (N_ACT, 2 * N_ACT), 1)
    spread = [jnp.where(col == j2 + b, 1.0, 0.0).astype(BF16) for b in range(2)]
    _split_rows(l_ref, tb, [jnp.dot(a, e, preferred_element_type=F32) for e in spread for a in (ahi, alo)])

    def compute(t, g_ref):
        t8 = pl.multiple_of(t * SPLIT_ROWS, SPLIT_ROWS)
        lhs = _load_tile(l_ref, t8).astype(BF16)
        for s in range(ROWS_PER_EXPERT):
            o_ref.at[s][pl.ds(t8, SPLIT_ROWS), :] = jnp.dot(lhs, _expert_weights(g_ref, s),
                                                             preferred_element_type=F32)
            yield

    _token_pipeline(tb, idx_ref, tab_ref, compute, bufs, compute_span=PEER_V_SPAN)
    o = [_merge_rows(o_ref, tb, k) for k in range(4)]
    out_ref[...] = x1_ref[...] + jnp.concatenate([o[0] + o[1], o[2] + o[3]], axis=1)


def _peer_v(idx, act, x1, vpk, tb):
    nb = idx.shape[0]
    return pl.pallas_call(
        functools.partial(_peer_v_body, tb=tb),
        grid=(nb,),
        in_specs=[pl.BlockSpec((None, tb, N_ACT), lambda i: (i, 0, 0), memory_space=pltpu.SMEM),
                  pl.BlockSpec((None, tb, N_ACT), lambda i: (i, 0, 0)),
                  pl.BlockSpec((tb, D_MODEL), lambda i: (i, 0)), _whole()],
        out_specs=pl.BlockSpec((tb, D_MODEL), lambda i: (i, 0)),
        out_shape=jax.ShapeDtypeStruct(x1.shape, F32),
        scratch_shapes=[pltpu.VMEM((N_ACT * ROWS_PER_EXPERT, LANES), jnp.int32)] * GATHER_BUFS
        + [_tile_scratch(tb, 2 * N_ACT), _tile_scratch(tb, D_MODEL // 2)],
        compiler_params=_cparams(("arbitrary",)),
        name="peer_v",
    )(idx, act, x1, vpk)


def _tile(n, pref):
    t = min(pref, n)
    while n % t:
        t //= 2
    return t


def kernel(x, mem, norm1_g, w_in, b_gate, b_forget, fox_q_g, fox_k_g, swa_q_g, swa_k_g, swa_sinks, mem_norm_g, w_mem_kv, mem_q_g, mem_k_g, w_fox_o, w_swa_o, w_mem_o, w_out, norm2_g, w_peer_q, peer_keys1, peer_keys2, peer_u, peer_v):
    b, s, d = x.shape
    n = b * s
    assert d == D_MODEL and s % 256 == 0 and mem.shape[1] == N_MEM
    depth = norm1_g.shape[0]
    xf = x.reshape(n, d)
    memf = mem.reshape(b * N_MEM, d)
    tb = BLOCK
    for l in range(depth):
        w = w_in[l]
        o_ff, o_sq, o_sk, o_sv, o_mq, o_gl = 1536, 1544, 2056, 2184, 2312, 2824
        w_att = jnp.concatenate(
            [w[:, 0:o_ff], w[:, o_sq:o_gl], w[:, o_ff:o_sq], jnp.zeros((d, LANES - FOX_HEADS), F32)],
            axis=1).astype(BF16)
        w_gate = w[:, o_gl:].astype(BF16)
        tile2 = lambda g: jnp.concatenate([g, g])
        gains = jnp.stack([tile2(fox_q_g[l]) * HEAD_D ** -0.5, tile2(fox_k_g[l]),
                           tile2(swa_q_g[l]) * HEAD_D ** -0.5, tile2(swa_k_g[l]),
                           mem_q_g[l] * MEM_HD ** -0.5] + [jnp.zeros((LANES,), F32)] * 3)
        bf = jnp.concatenate([b_forget[l], jnp.zeros((LANES - FOX_HEADS,), F32)]).reshape(1, LANES)
        g1 = norm1_g[l].reshape(1, d)

        fq, fk, fv, sq, sk, sv, mq, lf = _inproj(xf, g1, w_att, gains, bf, _tile(n, 512))
        ccol, crow = _cumsum(lf.reshape(b, s, LANES))
        o_fox = _fox(fq, fk, fv, ccol.reshape(n, LANES), crow.reshape(b * FOX_HEADS, 1, s), b, s, 256)
        o_swa = _swa(swa_sinks[l], sq, sk, sv, b, s)
        mk, mv = _memkv(memf, mem_norm_g[l].reshape(1, d), w_mem_kv[l].astype(BF16), mem_k_g[l].reshape(1, LANES))
        o_mem = _mem(mq, mk, mv, b, s, 256)

        x1, h2, qp = _merge(xf, o_fox, o_swa, o_mem, g1, w_gate, b_gate[l].reshape(1, 3 * d),
                            w_fox_o[l].astype(BF16), w_swa_o[l].astype(BF16), w_mem_o[l].astype(BF16),
                            w_out[l].astype(BF16), norm2_g[l].reshape(1, d), w_peer_q[l].astype(BF16), _tile(n, 256))

        e_t, gates_t = _topk(qp, peer_keys1[l].astype(BF16), peer_keys2[l].astype(BF16), tb)
        idx = _group_index_layout(jnp.swapaxes(e_t, 1, 2) * ROWS_PER_EXPERT, GATHER_BUFS // 2)
        act_t = _peer_u(idx, h2, gates_t, _pack_table(peer_u[l]), tb)
        xf = _peer_v(idx, act_t, x1, _pack_table(peer_v[l]), tb)
    return xf.reshape(b, s, d)
```

```python
import functools

import jax
import jax.numpy as jnp
from jax import lax
from jax.experimental import pallas as pl
from jax.experimental.pallas import tpu as pltpu

F32 = jnp.float32
BF16 = jnp.bfloat16

D_MODEL = 1024
BLOCK = 128
FOX_HEADS = 8
HEAD_D = 64
SWA_HEADS = 8
SWA_KV_HEADS = 2
MEM_HEADS = 4
MEM_HD = 128
N_MEM = 256
PEER_HEADS = 8
N_KEYS = 128
N_EXPERTS = N_KEYS * N_KEYS
PEER_TOPK = 16
N_ACT = PEER_HEADS * PEER_TOPK
EPS = 1e-6
NEG_INF = -1e30

LANES = 128
ROWS_PER_EXPERT = 4
VMEM_LIMIT = 48 * 1024 * 1024

C_FQ, C_FK, C_FV, C_SQ, C_SK, C_SV, C_MQ, C_FF = 0, 512, 1024, 1536, 2048, 2176, 2304, 2816
ATT_COLS = 2944


def _cparams(sem):
    return pltpu.CompilerParams(dimension_semantics=sem, vmem_limit_bytes=VMEM_LIMIT)


def _whole():
    return pl.BlockSpec(memory_space=pltpu.VMEM)


def _rms(xf, g):
    return xf * lax.rsqrt(jnp.mean(xf * xf, axis=-1, keepdims=True) + EPS) * g


def _seg_matrix(seg):
    r = (lax.broadcasted_iota(jnp.int32, (2 * LANES, LANES), 0) % LANES) // seg
    c = lax.broadcasted_iota(jnp.int32, (2 * LANES, LANES), 1) // seg
    return jnp.where(r == c, 1.0 / seg, 0.0).astype(BF16)


def _seg_norm(z, seg_mat, gain):
    sq = z * z
    hi = sq.astype(BF16)
    lo = (sq - hi.astype(F32)).astype(BF16)
    ms = jnp.dot(jnp.concatenate([hi, lo], axis=1), seg_mat, preferred_element_type=F32)
    return z * lax.rsqrt(ms + EPS) * gain


def _log_sigmoid(x):
    return jnp.minimum(x, 0.0) - jnp.log1p(jnp.exp(-jnp.abs(x)))


def _sigmoid(x):
    return 1.0 / (1.0 + jnp.exp(-x))


def _gelu(x):
    return 0.5 * x * (1.0 + lax.erf(x * 0.7071067811865476))


def _inproj_kernel(x_ref, g1_ref, w_ref, gains_ref, bf_ref,
                   fq_ref, fk_ref, fv_ref, sq_ref, sk_ref, sv_ref, mq_ref, lf_ref):
    h = _rms(x_ref[...], g1_ref[...]).astype(BF16)
    seg64 = _seg_matrix(HEAD_D)
    seg128 = _seg_matrix(MEM_HD)

    blocks = []
    for out_ref, nblk, seg_mat, gain_row in ((fq_ref, 4, seg64, 0), (fk_ref, 4, seg64, 1), (fv_ref, 4, None, None),
                                            (sq_ref, 4, seg64, 2), (sk_ref, 1, seg64, 3), (sv_ref, 1, None, None),
                                            (mq_ref, 4, seg128, 4)):
        blocks += [(out_ref, b, seg_mat, gain_row) for b in range(nblk)]
    blocks.append((lf_ref, 0, None, None))
    assert len(blocks) * LANES == ATT_COLS

    def proj(c0):
        width = min(2 * LANES, ATT_COLS - c0)
        return jnp.dot(h, w_ref[:, c0:c0 + width], preferred_element_type=F32)

    def finish(z, blk):
        out_ref, b, seg_mat, gain_row = blk
        if out_ref is lf_ref:
            out_ref[...] = _log_sigmoid(z + bf_ref[...])
        elif seg_mat is None:
            out_ref[:, b * LANES:(b + 1) * LANES] = z.astype(BF16)
        else:
            gain = gains_ref[gain_row:gain_row + 1, :]
            out_ref[:, b * LANES:(b + 1) * LANES] = _seg_norm(z, seg_mat, gain).astype(BF16)

    starts = list(range(0, ATT_COLS, 2 * LANES))
    z = proj(starts[0])
    for i, c0 in enumerate(starts):
        z_next = proj(starts[i + 1]) if i + 1 < len(starts) else None
        for half in range(z.shape[1] // LANES):
            finish(z[:, half * LANES:(half + 1) * LANES], blocks[c0 // LANES + half])
        z = z_next


def _inproj(xf, g1, w_att, gains, bf, tm):
    n = xf.shape[0]
    row = lambda w: pl.BlockSpec((tm, w), lambda i: (i, 0))
    out_shape = [jax.ShapeDtypeStruct((n, w), BF16) for w in (512, 512, 512, 512, 128, 128, 512)]
    out_shape.append(jax.ShapeDtypeStruct((n, 128), F32))
    return pl.pallas_call(
        _inproj_kernel,
        grid=(n // tm,),
        in_specs=[row(D_MODEL), _whole(), _whole(), _whole(), _whole()],
        out_specs=[row(512), row(512), row(512), row(512), row(128), row(128), row(512), row(128)],
        out_shape=out_shape,
        compiler_params=_cparams(("parallel",)),
        name="inproj",
    )(xf, g1, w_att, gains, bf)


def _cumsum_kernel(lf_ref, ccol_ref, crow_ref, *, blk):
    s = lf_ref.shape[0]
    r = lax.broadcasted_iota(jnp.int32, (blk, blk), 0)
    c = lax.broadcasted_iota(jnp.int32, (blk, blk), 1)
    tri = jnp.where(c <= r, 1.0, 0.0).astype(F32)
    carry = jnp.zeros((1, LANES), F32)
    for b in range(s // blk):
        x = lf_ref[b * blk:(b + 1) * blk, :]
        cs = jnp.dot(tri, x, preferred_element_type=F32, precision=lax.Precision.HIGHEST) + carry
        ccol_ref[b * blk:(b + 1) * blk, :] = cs
        carry = cs[blk - 1:blk, :]
    ct = jnp.transpose(ccol_ref[...])
    crow_ref[...] = ct[0:FOX_HEADS, :]


def _cumsum(lf3):
    b, s, _ = lf3.shape
    return pl.pallas_call(
        functools.partial(_cumsum_kernel, blk=256),
        grid=(b,),
        in_specs=[pl.BlockSpec((None, s, LANES), lambda i: (i, 0, 0))],
        out_specs=[pl.BlockSpec((None, s, LANES), lambda i: (i, 0, 0)),
                   pl.BlockSpec((None, FOX_HEADS, s), lambda i: (i, 0, 0))],
        out_shape=[jax.ShapeDtypeStruct((b, s, LANES), F32), jax.ShapeDtypeStruct((b, FOX_HEADS, s), F32)],
        compiler_params=_cparams(("parallel",)),
        name="cumsum",
    )(lf3)


def _fox_kernel(q_ref, k_ref, v_ref, ccol_ref, crow0_ref, crow1_ref, o_ref, m_scr, l_scr, acc_scr, *, tq):
    hp = pl.program_id(1)
    qi = pl.program_id(2)
    lane = lax.broadcasted_iota(jnp.int32, (tq, LANES), 1)
    q = q_ref[...]
    zero = jnp.zeros_like(q)
    qh = (jnp.where(lane < HEAD_D, q, zero), jnp.where(lane >= HEAD_D, q, zero))
    cc = ccol_ref[...]
    ci = tuple(jnp.sum(jnp.where(lane == 2 * hp + a, cc, 0.0), axis=1, keepdims=True) for a in range(2))
    crow = (crow0_ref, crow1_ref)
    m_scr[...] = jnp.full(m_scr.shape, NEG_INF, F32)
    l_scr[...] = jnp.zeros(l_scr.shape, F32)
    acc_scr[...] = jnp.zeros(acc_scr.shape, F32)
    rows = lax.broadcasted_iota(jnp.int32, (tq, tq), 0)
    cols = lax.broadcasted_iota(jnp.int32, (tq, tq), 1)

    def step(j, masked):
        ks = pl.multiple_of(j * tq, tq)
        kb = k_ref[pl.ds(ks, tq), :]
        vb = v_ref[pl.ds(ks, tq), :]
        for a in range(2):
            s = lax.dot_general(qh[a], kb, (((1,), (1,)), ((), ())), preferred_element_type=F32)
            s = s + (ci[a] - crow[a][:, pl.ds(ks, tq)])
            if masked:
                s = jnp.where(cols <= rows, s, NEG_INF)
            m_prev = m_scr[a]
            m_next = jnp.maximum(m_prev, jnp.max(s, axis=1, keepdims=True))
            p = jnp.exp(s - jnp.concatenate([m_next] * (tq // LANES), axis=1))
            alpha = jnp.exp(m_prev - m_next)
            l_scr[a] = alpha * l_scr[a] + jnp.sum(p, axis=1, keepdims=True)
            acc_scr[a] = alpha * acc_scr[a] + jnp.dot(p.astype(BF16), vb, preferred_element_type=F32)
            m_scr[a] = m_next

    def body(j, c):
        step(j, False)
        return c

    lax.fori_loop(0, qi, body, 0)
    step(qi, True)
    o0 = acc_scr[0] / l_scr[0]
    o1 = acc_scr[1] / l_scr[1]
    o_ref[...] = jnp.where(lane < HEAD_D, o0, o1).astype(BF16)


def _fox(fq, fk, fv, ccol, crow, b, s, tq):
    nq = s // tq
    qspec = pl.BlockSpec((tq, LANES), lambda bi, hp, qi: (bi * nq + qi, hp))
    kvspec = pl.BlockSpec((s, LANES), lambda bi, hp, qi: (bi, hp))
    return pl.pallas_call(
        functools.partial(_fox_kernel, tq=tq),
        grid=(b, FOX_HEADS // 2, nq),
        in_specs=[qspec, kvspec, kvspec,
                  pl.BlockSpec((tq, LANES), lambda bi, hp, qi: (bi * nq + qi, 0)),
                  pl.BlockSpec((None, 1, s), lambda bi, hp, qi: (bi * FOX_HEADS + 2 * hp, 0, 0)),
                  pl.BlockSpec((None, 1, s), lambda bi, hp, qi: (bi * FOX_HEADS + 2 * hp + 1, 0, 0))],
        out_specs=qspec,
        out_shape=jax.ShapeDtypeStruct(fq.shape, BF16),
        scratch_shapes=[pltpu.VMEM((2, tq, LANES), F32), pltpu.VMEM((2, tq, LANES), F32),
                        pltpu.VMEM((2, tq, LANES), F32)],
        compiler_params=_cparams(("parallel", "parallel", "arbitrary")),
        name="fox",
    )(fq, fk, fv, ccol, crow, crow)


def _swa_kernel(sink_ref, q_ref, kp_ref, kc_ref, vp_ref, vc_ref, o_ref):
    blk = pl.program_id(1)
    lane = lax.broadcasted_iota(jnp.int32, (BLOCK, LANES), 1)
    kb = jnp.concatenate([kp_ref[...], kc_ref[...]], axis=0)
    vb = jnp.concatenate([vp_ref[...], vc_ref[...]], axis=0)
    qi = lax.broadcasted_iota(jnp.int32, (BLOCK, 2 * BLOCK), 0)
    kj = lax.broadcasted_iota(jnp.int32, (BLOCK, 2 * BLOCK), 1)
    dist = qi + BLOCK - kj
    ok = (dist >= 0) & (dist < BLOCK) & ((blk > 0) | (kj >= BLOCK))
    distf = dist.astype(F32)
    rep = SWA_HEADS // SWA_KV_HEADS
    for pair in range(SWA_HEADS // 2):
        q128 = q_ref[:, pair * LANES:(pair + 1) * LANES]
        outs = []
        for half in range(2):
            h = 2 * pair + half
            grp = h // rep
            qa = q128 if half == grp else pltpu.roll(q128, HEAD_D, axis=1)
            in_grp = (lane >= grp * HEAD_D) & (lane < (grp + 1) * HEAD_D)
            qa = jnp.where(in_grp, qa, jnp.zeros_like(qa))
            s = lax.dot_general(qa, kb, (((1,), (1,)), ((), ())), preferred_element_type=F32)
            s = s - (2.0 ** -(h + 1)) * distf
            s = jnp.where(ok, s, NEG_INF)
            sink = sink_ref[h]
            m = jnp.maximum(jnp.max(s, axis=1, keepdims=True), sink)
            p = jnp.exp(s - m)
            den = jnp.sum(p, axis=1, keepdims=True) + jnp.exp(sink - m)
            o = jnp.dot(p.astype(BF16), vb, preferred_element_type=F32) / den
            outs.append(o if half == grp else pltpu.roll(o, HEAD_D, axis=1))
        o_ref[:, pair * LANES:(pair + 1) * LANES] = jnp.where(lane < HEAD_D, outs[0], outs[1]).astype(BF16)


def _swa(sinks, sq, sk, sv, b, s):
    nb = s // BLOCK
    cur = lambda bi, i: (bi * nb + i, 0)
    prev = lambda bi, i: (bi * nb + jnp.maximum(i - 1, 0), 0)
    return pl.pallas_call(
        _swa_kernel,
        grid=(b, nb),
        in_specs=[pl.BlockSpec(memory_space=pltpu.SMEM),
                  pl.BlockSpec((BLOCK, 512), cur),
                  pl.BlockSpec((BLOCK, LANES), prev), pl.BlockSpec((BLOCK, LANES), cur),
                  pl.BlockSpec((BLOCK, LANES), prev), pl.BlockSpec((BLOCK, LANES), cur)],
        out_specs=pl.BlockSpec((BLOCK, 512), cur),
        out_shape=jax.ShapeDtypeStruct(sq.shape, BF16),
        compiler_params=_cparams(("parallel", "parallel")),
        name="swa",
    )(sinks, sq, sk, sk, sv, sv)


def _memkv_kernel(mem_ref, g_ref, w_ref, gk_ref, mk_ref, mv_ref):
    mn = _rms(mem_ref[...], g_ref[...]).astype(BF16)
    seg128 = _seg_matrix(MEM_HD)
    width = MEM_HEADS * MEM_HD
    for hd in range(MEM_HEADS):
        z = jnp.dot(mn, w_ref[:, hd * LANES:(hd + 1) * LANES], preferred_element_type=F32)
        mk_ref[:, hd * LANES:(hd + 1) * LANES] = _seg_norm(z, seg128, gk_ref[...]).astype(BF16)
    mv_ref[...] = jnp.dot(mn, w_ref[:, width:2 * width], preferred_element_type=F32).astype(BF16)


def _memkv(memf, g, w, gk):
    n = memf.shape[0]
    width = MEM_HEADS * MEM_HD
    tm = N_MEM
    return pl.pallas_call(
        _memkv_kernel,
        grid=(n // tm,),
        in_specs=[pl.BlockSpec((tm, D_MODEL), lambda i: (i, 0)), _whole(), _whole(), _whole()],
        out_specs=[pl.BlockSpec((tm, width), lambda i: (i, 0)), pl.BlockSpec((tm, width), lambda i: (i, 0))],
        out_shape=[jax.ShapeDtypeStruct((n, width), BF16), jax.ShapeDtypeStruct((n, width), BF16)],
        compiler_params=_cparams(("parallel",)),
        name="memkv",
    )(memf, g, w, gk)


def _mem_kernel(q_ref, k_ref, v_ref, o_ref):
    for hd in range(MEM_HEADS):
        sl = slice(hd * LANES, (hd + 1) * LANES)
        s = lax.dot_general(q_ref[:, sl], k_ref[:, sl], (((1,), (1,)), ((), ())), preferred_element_type=F32)
        m = jnp.max(s, axis=1, keepdims=True)
        p = jnp.exp(s - m)
        den = jnp.sum(p, axis=1, keepdims=True)
        o_ref[:, sl] = (jnp.dot(p.astype(BF16), v_ref[:, sl], preferred_element_type=F32) / den).astype(BF16)


def _mem(mq, mk, mv, b, s, tq):
    nq = s // tq
    width = MEM_HEADS * MEM_HD
    qspec = pl.BlockSpec((tq, width), lambda bi, qi: (bi * nq + qi, 0))
    kvspec = pl.BlockSpec((N_MEM, width), lambda bi, qi: (bi, 0))
    return pl.pallas_call(
        _mem_kernel,
        grid=(b, nq),
        in_specs=[qspec, kvspec, kvspec],
        out_specs=qspec,
        out_shape=jax.ShapeDtypeStruct(mq.shape, BF16),
        compiler_params=_cparams(("parallel", "parallel")),
        name="mem",
    )(mq, mk, mv)


def _merge_kernel(x_ref, of_ref, os_ref, om_ref, g1_ref, wg_ref, bg_ref, wfo_ref, wso_ref, wmo_ref, wout_ref,
                  g2_ref, wpq_ref, x1_ref, h2_ref, qp_ref):
    x = x_ref[...]
    h = _rms(x, g1_ref[...]).astype(BF16)
    merged = jnp.zeros(x.shape, F32)
    for br, (o_ref, w_ref) in enumerate(((of_ref, wfo_ref), (os_ref, wso_ref), (om_ref, wmo_ref))):
        sl = slice(br * D_MODEL, (br + 1) * D_MODEL)
        gate = _sigmoid(jnp.dot(h, wg_ref[:, sl], preferred_element_type=F32) + bg_ref[:, sl])
        merged = merged + gate * jnp.dot(o_ref[...], w_ref[...], preferred_element_type=F32)
    x1 = x + jnp.dot(merged.astype(BF16), wout_ref[...], preferred_element_type=F32)
    x1_ref[...] = x1
    h2 = _rms(x1, g2_ref[...])
    h2_ref[...] = h2
    qp_ref[...] = jnp.dot(h2.astype(BF16), wpq_ref[...], preferred_element_type=F32).astype(BF16)


def _merge(xf, of, os_, om, g1, wg, bg, wfo, wso, wmo, wout, g2, wpq, tm):
    n = xf.shape[0]
    row = lambda w: pl.BlockSpec((tm, w), lambda i: (i, 0))
    qw = wpq.shape[1]
    return pl.pallas_call(
        _merge_kernel,
        grid=(n // tm,),
        in_specs=[row(D_MODEL), row(512), row(512), row(512)] + [_whole()] * 9,
        out_specs=[row(D_MODEL), row(D_MODEL), row(qw)],
        out_shape=[jax.ShapeDtypeStruct((n, D_MODEL), F32), jax.ShapeDtypeStruct((n, D_MODEL), F32),
                   jax.ShapeDtypeStruct((n, qw), BF16)],
        compiler_params=_cparams(("parallel",)),
        name="merge",
    )(xf, of, os_, om, g1, wg, bg, wfo, wso, wmo, wout, g2, wpq)


def _topk_rows(s, k, order, payload=None):
    big = jnp.int32(2 ** 30)
    vals, sels = [], []
    for _ in range(k):
        m = jnp.max(s, axis=0, keepdims=True)
        sel = jnp.min(jnp.where(s == m, order, big), axis=0, keepdims=True)
        hit = order == sel
        vals.append(m)
        if payload is None:
            sels.append(sel)
        else:
            sels.append(jnp.max(jnp.where(hit, payload, -1), axis=0, keepdims=True))
        s = jnp.where(hit, -jnp.inf, s)
    return jnp.concatenate(vals, axis=0), jnp.concatenate(sels, axis=0)


def _pair_candidates(v1, i1, v2, i2):
    t = v1.shape[1]
    sub = lax.broadcasted_iota(jnp.int32, (8, t), 0)
    low = sub < 4
    b_lo = jnp.where(low, sub, sub - 4)
    v2a, i2a = v2[0:8], i2[0:8]
    v2r, i2r = pltpu.roll(v2a, 4, axis=0), pltpu.roll(i2a, 4, axis=0)
    vals, flat, eidx = [], [], []

    def single(a, v2x, i2x, b0, nvalid):
        vals.append(jnp.where(sub < nvalid, v1[a:a + 1] + v2x, -jnp.inf))
        flat.append(a * PEER_TOPK + b0 + sub)
        eidx.append(i1[a:a + 1] * N_KEYS + i2x)

    def double(a, nvalid_a, nvalid_b):
        va = jnp.where(low, v1[a:a + 1], v1[a + 1:a + 2]) + jnp.where(low, v2a, v2r)
        ok = b_lo < jnp.where(low, nvalid_a, nvalid_b)
        vals.append(jnp.where(ok, va, -jnp.inf))
        flat.append(jnp.where(low, a * PEER_TOPK, (a + 1) * PEER_TOPK) + b_lo)
        eidx.append(jnp.where(low, i1[a:a + 1], i1[a + 1:a + 2]) * N_KEYS + jnp.where(low, i2a, i2r))

    single(0, v2a, i2a, 0, 8)
    single(0, v2[8:16], i2[8:16], 8, 8)
    single(1, v2a, i2a, 0, 8)
    single(2, v2a, i2a, 0, 5)
    single(3, v2a, i2a, 0, 4)
    double(4, 3, 2)
    double(6, 2, 2)
    vals.append(v1[8:16] + v2[0:1])
    flat.append((8 + sub) * PEER_TOPK)
    eidx.append(i1[8:16] * N_KEYS + i2[0:1])
    cat = lambda xs: jnp.concatenate(xs, axis=0)
    return cat(vals), cat(flat), cat(eidx)


def _topk_kernel(qp_ref, k1_ref, k2_ref, e_ref, g_ref):
    nt = (((1,), (1,)), ((), ()))
    rid = lax.broadcasted_iota(jnp.int32, (N_KEYS, qp_ref.shape[0]), 0)
    for hd in range(PEER_HEADS):
        c0 = hd * 2 * N_KEYS
        s1 = lax.dot_general(k1_ref[hd], qp_ref[:, c0:c0 + N_KEYS], nt, preferred_element_type=F32)
        s2 = lax.dot_general(k2_ref[hd], qp_ref[:, c0 + N_KEYS:c0 + 2 * N_KEYS], nt, preferred_element_type=F32)
        v1, i1 = _topk_rows(s1, PEER_TOPK, rid)
        v2, i2 = _topk_rows(s2, PEER_TOPK, rid)
        cand, flat, eidx = _pair_candidates(v1, i1, v2, i2)
        sv, ex = _topk_rows(cand, PEER_TOPK, flat, payload=eidx)
        p = jnp.exp(sv - sv[0:1, :])
        g_ref[hd * PEER_TOPK:(hd + 1) * PEER_TOPK, :] = p / jnp.sum(p, axis=0, keepdims=True)
        e_ref[hd * PEER_TOPK:(hd + 1) * PEER_TOPK, :] = ex


def _topk(qp, k1, k2, tb):
    n = qp.shape[0]
    nb = n // tb
    oblk = pl.BlockSpec((None, N_ACT, tb), lambda i: (i, 0, 0))
    return pl.pallas_call(
        _topk_kernel,
        grid=(nb,),
        in_specs=[pl.BlockSpec((tb, qp.shape[1]), lambda i: (i, 0)), _whole(), _whole()],
        out_specs=[oblk, oblk],
        out_shape=[jax.ShapeDtypeStruct((nb, N_ACT, tb), jnp.int32), jax.ShapeDtypeStruct((nb, N_ACT, tb), F32)],
        compiler_params=_cparams(("parallel",)),
        name="topk",
    )(qp, k1, k2)


def _pack_table(w):
    wb = w.astype(BF16)
    half = w.shape[1] // 2
    lo = lax.bitcast_convert_type(wb[:, :half], jnp.uint16).astype(jnp.uint32)
    hi = lax.bitcast_convert_type(wb[:, half:], jnp.uint16).astype(jnp.uint32)
    pk = lax.bitcast_convert_type(lo | (hi << 16), jnp.int32)
    return pk.reshape(w.shape[0] * ROWS_PER_EXPERT, LANES)


GATHER_BUFS = 8
SPLIT_ROWS = 8
NT_DIMS = (((1,), (1,)), ((), ()))


def _group_index_layout(e_tok, grp):
    nb, tb, n_act = e_tok.shape
    seg = n_act // grp
    x = e_tok.reshape(nb, tb // grp, grp, grp, seg)
    return jnp.swapaxes(x, 2, 3).reshape(nb, tb, n_act)


GATHER_CHUNKS = 4
PEER_U_SPAN = 1.0
PEER_V_SPAN = 0.6


def _gather_rows(idx_ref, tab_ref, g_ref, g0, k, grp, chunk=None):
    seg = N_ACT // grp
    rows = [idx_ref.at[g0 + kk] for kk in range(grp)]
    per = seg // GATHER_CHUNKS
    jos = range(seg) if chunk is None else range(chunk * per, (chunk + 1) * per)
    for jo in jos:
        for kk in range(grp):
            j = kk * seg + jo
            e4 = pl.multiple_of(rows[kk][k * seg + jo], ROWS_PER_EXPERT)
            g_ref[ROWS_PER_EXPERT * j:ROWS_PER_EXPERT * (j + 1), :] = tab_ref[pl.ds(e4, ROWS_PER_EXPERT), :]


def _token_pipeline(tb, idx_ref, tab_ref, compute, bufs, compute_span):
    grp = len(bufs) // 2
    sets = (bufs[:grp], bufs[grp:])
    for k in range(grp):
        _gather_rows(idx_ref, tab_ref, sets[0][k], 0, k, grp)

    def phase(src, dst, t0):
        nxt = jnp.minimum(t0 + grp, tb - grp)
        steps = [step for k in range(grp) for step in [compute(t0 + k, src[k])] * ROWS_PER_EXPERT]
        n_gather = grp * GATHER_CHUNKS
        done = 0
        for i in range(n_gather):
            while done < len(steps) and int(done * compute_span * n_gather / len(steps)) <= i:
                next(steps[done], None)
                done += 1
            _gather_rows(idx_ref, tab_ref, dst[i // GATHER_CHUNKS], nxt, i // GATHER_CHUNKS, grp, i % GATHER_CHUNKS)
        for gen in steps[done:]:
            next(gen, None)

    def body(it, c):
        t0 = it * grp
        for parity in range(2):
            @pl.when(it % 2 == parity)
            def _():
                phase(sets[parity], sets[1 - parity], t0)
        return c

    lax.fori_loop(0, tb // grp, body, 0)


def _expert_weights(g_ref, s):
    return pltpu.bitcast(g_ref[pl.ds(s, N_ACT, stride=ROWS_PER_EXPERT), :], BF16)


def _split_rows(ref, tb, rows):
    @pl.when(pl.program_id(0) == 0)
    def _():
        ref[...] = jnp.zeros(ref.shape, ref.dtype)

    for r, val in enumerate(rows):
        for c in range(ref.shape[0]):
            ref.at[c][pl.ds(r, tb, stride=SPLIT_ROWS), :] = val[:, c * LANES:(c + 1) * LANES]


def _merge_rows(ref, tb, r):
    return jnp.concatenate([ref.at[c][pl.ds(r, tb, stride=SPLIT_ROWS), :] for c in range(ref.shape[0])], axis=1)


def _load_tile(ref, t8):
    return jnp.concatenate([ref.at[c][pl.ds(t8, SPLIT_ROWS), :] for c in range(ref.shape[0])], axis=1)


def _store_tile(ref, t8, val):
    for c in range(ref.shape[0]):
        ref.at[c][pl.ds(t8, SPLIT_ROWS), :] = val[:, c * LANES:(c + 1) * LANES]


def _tile_scratch(tb, width):
    return pltpu.VMEM((width // LANES, tb * SPLIT_ROWS, LANES), F32)


def _hi_lo(x):
    hi = x.astype(BF16).astype(F32)
    return hi, x - hi


def _peer_u_body(idx_ref, h_ref, gate_ref, tab_ref, act_ref, *scratch, tb):
    bufs, (hl_ref, r_ref) = scratch[:GATHER_BUFS], scratch[GATHER_BUFS:]
    half = D_MODEL // 2
    hhi, hlo = _hi_lo(h_ref[...])
    _split_rows(hl_ref, tb, (hhi[:, :half], hlo[:, :half], hhi[:, half:], hlo[:, half:]))

    def compute(t, g_ref):
        t8 = pl.multiple_of(t * SPLIT_ROWS, SPLIT_ROWS)
        lhs = _load_tile(hl_ref, t8).astype(BF16)
        acc = jnp.zeros((SPLIT_ROWS, 2 * N_ACT), F32)
        for s in range(ROWS_PER_EXPERT):
            acc = acc + lax.dot_general(lhs[:, s * LANES:(s + 1) * LANES], _expert_weights(g_ref, s), NT_DIMS,
                                        preferred_element_type=F32)
            if s == ROWS_PER_EXPERT - 1:
                _store_tile(r_ref, t8, acc)
            yield

    _token_pipeline(tb, idx_ref, tab_ref, compute, bufs, compute_span=PEER_U_SPAN)
    r = [_merge_rows(r_ref, tb, k) for k in range(4)]
    col = lax.broadcasted_iota(jnp.int32, (tb, 2 * N_ACT), 1)
    rc = jnp.where(col % 2 == 0, r[0] + r[1], r[2] + r[3])
    pair = jnp.where(lax.broadcasted_iota(jnp.int32, (2 * N_ACT, N_ACT), 0) // 2
                     == lax.broadcasted_iota(jnp.int32, (2 * N_ACT, N_ACT), 1), 1.0, 0.0).astype(F32)
    a = jnp.dot(rc, pair, preferred_element_type=F32, precision=lax.Precision.HIGHEST)
    act_ref[...] = _gelu(a) * jnp.transpose(gate_ref[...])


def _peer_u(idx, h2, gates_t, upk, tb):
    nb = idx.shape[0]
    return pl.pallas_call(
        functools.partial(_peer_u_body, tb=tb),
        grid=(nb,),
        in_specs=[pl.BlockSpec((None, tb, N_ACT), lambda i: (i, 0, 0), memory_space=pltpu.SMEM),
                  pl.BlockSpec((tb, D_MODEL), lambda i: (i, 0)),
                  pl.BlockSpec((None, N_ACT, tb), lambda i: (i, 0, 0)), _whole()],
        out_specs=pl.BlockSpec((None, tb, N_ACT), lambda i: (i, 0, 0)),
        out_shape=jax.ShapeDtypeStruct((nb, tb, N_ACT), F32),
        scratch_shapes=[pltpu.VMEM((N_ACT * ROWS_PER_EXPERT, LANES), jnp.int32)] * GATHER_BUFS
        + [_tile_scratch(tb, D_MODEL // 2), _tile_scratch(tb, 2 * N_ACT)],
        compiler_params=_cparams(("arbitrary",)),
        name="peer_u",
    )(idx, h2, gates_t, upk)


def _peer_v_body(idx_ref, act_ref, x1_ref, tab_ref, out_ref, *scratch, tb):
    bufs, (l_ref, o_ref) = scratch[:GATHER_BUFS], scratch[GATHER_BUFS:]
    ahi, alo = _hi_lo(act_ref[...])
    ahi, alo = ahi.astype(BF16), alo.astype(BF16)
    j2 = 2 * lax.broadcasted_iota(jnp.int32, (N_ACT, 2 * N_ACT), 0)
    col = lax.broadcasted_iota(jnp.int32, (N_ACT, 2 * N_ACT), 1)
    spread = [jnp.where(col == j2 + b, 1.0, 0.0).astype(BF16) for b in range(2)]
    _split_rows(l_ref, tb, [jnp.dot(a, e, preferred_element_type=F32) for e in spread for a in (ahi, alo)])

    def compute(t, g_ref):
        t8 = pl.multiple_of(t * SPLIT_ROWS, SPLIT_ROWS)
        lhs = _load_tile(l_ref, t8).astype(BF16)
        for s in range(ROWS_PER_EXPERT):
            o_ref.at[s][pl.ds(t8, SPLIT_ROWS), :] = jnp.dot(lhs, _expert_weights(g_ref, s),
                                                             preferred_element_type=F32)
            yield

    _token_pipeline(tb, idx_ref, tab_ref, compute, bufs, compute_span=PEER_V_SPAN)
    o = [_merge_rows(o_ref, tb, k) for k in range(4)]
    out_ref[...] = x1_ref[...] + jnp.concatenate([o[0] + o[1], o[2] + o[3]], axis=1)


def _peer_v(idx, act, x1, vpk, tb):
    nb = idx.shape[0]
    return pl.pallas_call(
        functools.partial(_peer_v_body, tb=tb),
        grid=(nb,),
        in_specs=[pl.BlockSpec((None, tb, N_ACT), lambda i: (i, 0, 0), memory_space=pltpu.SMEM),
                  pl.BlockSpec((None, tb, N_ACT), lambda i: (i, 0, 0)),
                  pl.BlockSpec((tb, D_MODEL), lambda i: (i, 0)), _whole()],
        out_specs=pl.BlockSpec((tb, D_MODEL), lambda i: (i, 0)),
        out_shape=jax.ShapeDtypeStruct(x1.shape, F32),
        scratch_shapes=[pltpu.VMEM((N_ACT * ROWS_PER_EXPERT, LANES), jnp.int32)] * GATHER_BUFS
        + [_tile_scratch(tb, 2 * N_ACT), _tile_scratch(tb, D_MODEL // 2)],
        compiler_params=_cparams(("arbitrary",)),
        name="peer_v",
    )(idx, act, x1, vpk)


def _tile(n, pref):
    t = min(pref, n)
    while n % t:
        t //= 2
    return t


def kernel(x, mem, norm1_g, w_in, b_gate, b_forget, fox_q_g, fox_k_g, swa_q_g, swa_k_g, swa_sinks, mem_norm_g, w_mem_kv, mem_q_g, mem_k_g, w_fox_o, w_swa_o, w_mem_o, w_out, norm2_g, w_peer_q, peer_keys1, peer_keys2, peer_u, peer_v):
    b, s, d = x.shape
    n = b * s
    assert d == D_MODEL and s % 256 == 0 and mem.shape[1] == N_MEM
    depth = norm1_g.shape[0]
    xf = x.reshape(n, d)
    memf = mem.reshape(b * N_MEM, d)
    tb = BLOCK
    for l in range(depth):
        w = w_in[l]
        o_ff, o_sq, o_sk, o_sv, o_mq, o_gl = 1536, 1544, 2056, 2184, 2312, 2824
        w_att = jnp.concatenate(
            [w[:, 0:o_ff], w[:, o_sq:o_gl], w[:, o_ff:o_sq], jnp.zeros((d, LANES - FOX_HEADS), F32)],
            axis=1).astype(BF16)
        w_gate = w[:, o_gl:].astype(BF16)
        tile2 = lambda g: jnp.concatenate([g, g])
        gains = jnp.stack([tile2(fox_q_g[l]) * HEAD_D ** -0.5, tile2(fox_k_g[l]),
                           tile2(swa_q_g[l]) * HEAD_D ** -0.5, tile2(swa_k_g[l]),
                           mem_q_g[l] * MEM_HD ** -0.5] + [jnp.zeros((LANES,), F32)] * 3)
        bf = jnp.concatenate([b_forget[l], jnp.zeros((LANES - FOX_HEADS,), F32)]).reshape(1, LANES)
        g1 = norm1_g[l].reshape(1, d)

        fq, fk, fv, sq, sk, sv, mq, lf = _inproj(xf, g1, w_att, gains, bf, _tile(n, 512))
        ccol, crow = _cumsum(lf.reshape(b, s, LANES))
        o_fox = _fox(fq, fk, fv, ccol.reshape(n, LANES), crow.reshape(b * FOX_HEADS, 1, s), b, s, 512)
        o_swa = _swa(swa_sinks[l], sq, sk, sv, b, s)
        mk, mv = _memkv(memf, mem_norm_g[l].reshape(1, d), w_mem_kv[l].astype(BF16), mem_k_g[l].reshape(1, LANES))
        o_mem = _mem(mq, mk, mv, b, s, 256)

        x1, h2, qp = _merge(xf, o_fox, o_swa, o_mem, g1, w_gate, b_gate[l].reshape(1, 3 * d),
                            w_fox_o[l].astype(BF16), w_swa_o[l].astype(BF16), w_mem_o[l].astype(BF16),
                            w_out[l].astype(BF16), norm2_g[l].reshape(1, d), w_peer_q[l].astype(BF16), _tile(n, 256))

        e_t, gates_t = _topk(qp, peer_keys1[l].astype(BF16), peer_keys2[l].astype(BF16), tb)
        idx = _group_index_layout(jnp.swapaxes(e_t, 1, 2) * ROWS_PER_EXPERT, GATHER_BUFS // 2)
        act_t = _peer_u(idx, h2, gates_t, _pack_table(peer_u[l]), tb)
        xf = _peer_v(idx, act_t, x1, _pack_table(peer_v[l]), tb)
    return xf.reshape(b, s, d)
```

```python
import functools

import jax
import jax.numpy as jnp
from jax import lax
from jax.experimental import pallas as pl
from jax.experimental.pallas import tpu as pltpu

F32 = jnp.float32
BF16 = jnp.bfloat16

D_MODEL = 1024
BLOCK = 128
FOX_HEADS = 8
HEAD_D = 64
SWA_HEADS = 8
SWA_KV_HEADS = 2
MEM_HEADS = 4
MEM_HD = 128
N_MEM = 256
PEER_HEADS = 8
N_KEYS = 128
N_EXPERTS = N_KEYS * N_KEYS
PEER_TOPK = 16
N_ACT = PEER_HEADS * PEER_TOPK
EPS = 1e-6
NEG_INF = -1e30

LANES = 128
ROWS_PER_EXPERT = 4
VMEM_LIMIT = 48 * 1024 * 1024

C_FQ, C_FK, C_FV, C_SQ, C_SK, C_SV, C_MQ, C_FF = 0, 512, 1024, 1536, 2048, 2176, 2304, 2816
ATT_COLS = 2944


def _cparams(sem):
    return pltpu.CompilerParams(dimension_semantics=sem, vmem_limit_bytes=VMEM_LIMIT)


def _whole():
    return pl.BlockSpec(memory_space=pltpu.VMEM)


def _rms(xf, g):
    return xf * lax.rsqrt(jnp.mean(xf * xf, axis=-1, keepdims=True) + EPS) * g


def _seg_matrix(seg):
    r = (lax.broadcasted_iota(jnp.int32, (2 * LANES, LANES), 0) % LANES) // seg
    c = lax.broadcasted_iota(jnp.int32, (2 * LANES, LANES), 1) // seg
    return jnp.where(r == c, 1.0 / seg, 0.0).astype(BF16)


def _seg_norm(z, seg_mat, gain):
    sq = z * z
    hi = sq.astype(BF16)
    lo = (sq - hi.astype(F32)).astype(BF16)
    ms = jnp.dot(jnp.concatenate([hi, lo], axis=1), seg_mat, preferred_element_type=F32)
    return z * lax.rsqrt(ms + EPS) * gain


def _log_sigmoid(x):
    return jnp.minimum(x, 0.0) - jnp.log1p(jnp.exp(-jnp.abs(x)))


def _sigmoid(x):
    return 1.0 / (1.0 + jnp.exp(-x))


def _gelu(x):
    return 0.5 * x * (1.0 + lax.erf(x * 0.7071067811865476))


def _inproj_kernel(x_ref, g1_ref, w_ref, gains_ref, bf_ref,
                   fq_ref, fk_ref, fv_ref, sq_ref, sk_ref, sv_ref, mq_ref, lf_ref):
    h = _rms(x_ref[...], g1_ref[...]).astype(BF16)
    seg64 = _seg_matrix(HEAD_D)
    seg128 = _seg_matrix(MEM_HD)

    blocks = []
    for out_ref, nblk, seg_mat, gain_row in ((fq_ref, 4, seg64, 0), (fk_ref, 4, seg64, 1), (fv_ref, 4, None, None),
                                            (sq_ref, 4, seg64, 2), (sk_ref, 1, seg64, 3), (sv_ref, 1, None, None),
                                            (mq_ref, 4, seg128, 4)):
        blocks += [(out_ref, b, seg_mat, gain_row) for b in range(nblk)]
    blocks.append((lf_ref, 0, None, None))
    assert len(blocks) * LANES == ATT_COLS

    def proj(c0):
        width = min(2 * LANES, ATT_COLS - c0)
        return jnp.dot(h, w_ref[:, c0:c0 + width], preferred_element_type=F32)

    def finish(z, blk):
        out_ref, b, seg_mat, gain_row = blk
        if out_ref is lf_ref:
            out_ref[...] = _log_sigmoid(z + bf_ref[...])
        elif seg_mat is None:
            out_ref[:, b * LANES:(b + 1) * LANES] = z.astype(BF16)
        else:
            gain = gains_ref[gain_row:gain_row + 1, :]
            out_ref[:, b * LANES:(b + 1) * LANES] = _seg_norm(z, seg_mat, gain).astype(BF16)

    starts = list(range(0, ATT_COLS, 2 * LANES))
    z = proj(starts[0])
    for i, c0 in enumerate(starts):
        z_next = proj(starts[i + 1]) if i + 1 < len(starts) else None
        for half in range(z.shape[1] // LANES):
            finish(z[:, half * LANES:(half + 1) * LANES], blocks[c0 // LANES + half])
        z = z_next


def _inproj(xf, g1, w_att, gains, bf, tm):
    n = xf.shape[0]
    row = lambda w: pl.BlockSpec((tm, w), lambda i: (i, 0))
    out_shape = [jax.ShapeDtypeStruct((n, w), BF16) for w in (512, 512, 512, 512, 128, 128, 512)]
    out_shape.append(jax.ShapeDtypeStruct((n, 128), F32))
    return pl.pallas_call(
        _inproj_kernel,
        grid=(n // tm,),
        in_specs=[row(D_MODEL), _whole(), _whole(), _whole(), _whole()],
        out_specs=[row(512), row(512), row(512), row(512), row(128), row(128), row(512), row(128)],
        out_shape=out_shape,
        compiler_params=_cparams(("parallel",)),
        name="inproj",
    )(xf, g1, w_att, gains, bf)


def _cumsum_kernel(lf_ref, ccol_ref, crow_ref, *, blk):
    s = lf_ref.shape[0]
    r = lax.broadcasted_iota(jnp.int32, (blk, blk), 0)
    c = lax.broadcasted_iota(jnp.int32, (blk, blk), 1)
    tri = jnp.where(c <= r, 1.0, 0.0).astype(F32)
    carry = jnp.zeros((1, LANES), F32)
    for b in range(s // blk):
        x = lf_ref[b * blk:(b + 1) * blk, :]
        cs = jnp.dot(tri, x, preferred_element_type=F32, precision=lax.Precision.HIGHEST) + carry
        ccol_ref[b * blk:(b + 1) * blk, :] = cs
        carry = cs[blk - 1:blk, :]
    ct = jnp.transpose(ccol_ref[...])
    crow_ref[...] = ct[0:FOX_HEADS, :]


def _cumsum(lf3):
    b, s, _ = lf3.shape
    return pl.pallas_call(
        functools.partial(_cumsum_kernel, blk=256),
        grid=(b,),
        in_specs=[pl.BlockSpec((None, s, LANES), lambda i: (i, 0, 0))],
        out_specs=[pl.BlockSpec((None, s, LANES), lambda i: (i, 0, 0)),
                   pl.BlockSpec((None, FOX_HEADS, s), lambda i: (i, 0, 0))],
        out_shape=[jax.ShapeDtypeStruct((b, s, LANES), F32), jax.ShapeDtypeStruct((b, FOX_HEADS, s), F32)],
        compiler_params=_cparams(("parallel",)),
        name="cumsum",
    )(lf3)


def _fox_kernel(q_ref, k_ref, v_ref, ccol_ref, crow0_ref, crow1_ref, o_ref, m_scr, l_scr, acc_scr, *, tq):
    hp = pl.program_id(1)
    qi = pl.program_id(2)
    lane = lax.broadcasted_iota(jnp.int32, (tq, LANES), 1)
    q = q_ref[...]
    zero = jnp.zeros_like(q)
    qh = (jnp.where(lane < HEAD_D, q, zero), jnp.where(lane >= HEAD_D, q, zero))
    cc = ccol_ref[...]
    ci = tuple(jnp.sum(jnp.where(lane == 2 * hp + a, cc, 0.0), axis=1, keepdims=True) for a in range(2))
    crow = (crow0_ref, crow1_ref)
    m_scr[...] = jnp.full(m_scr.shape, NEG_INF, F32)
    l_scr[...] = jnp.zeros(l_scr.shape, F32)
    acc_scr[...] = jnp.zeros(acc_scr.shape, F32)
    rows = lax.broadcasted_iota(jnp.int32, (tq, tq), 0)
    cols = lax.broadcasted_iota(jnp.int32, (tq, tq), 1)

    def step(j, masked):
        ks = pl.multiple_of(j * tq, tq)
        kb = k_ref[pl.ds(ks, tq), :]
        vb = v_ref[pl.ds(ks, tq), :]
        for a in range(2):
            s = lax.dot_general(qh[a], kb, (((1,), (1,)), ((), ())), preferred_element_type=F32)
            s = s + (ci[a] - crow[a][:, pl.ds(ks, tq)])
            if masked:
                s = jnp.where(cols <= rows, s, NEG_INF)
            m_prev = m_scr[a]
            m_next = jnp.maximum(m_prev, jnp.max(s, axis=1, keepdims=True))
            p = jnp.exp(s - jnp.concatenate([m_next] * (tq // LANES), axis=1))
            alpha = jnp.exp(m_prev - m_next)
            l_scr[a] = alpha * l_scr[a] + jnp.sum(p, axis=1, keepdims=True)
            acc_scr[a] = alpha * acc_scr[a] + jnp.dot(p.astype(BF16), vb, preferred_element_type=F32)
            m_scr[a] = m_next

    def body(j, c):
        step(j, False)
        return c

    lax.fori_loop(0, qi, body, 0)
    step(qi, True)
    o0 = acc_scr[0] / l_scr[0]
    o1 = acc_scr[1] / l_scr[1]
    o_ref[...] = jnp.where(lane < HEAD_D, o0, o1).astype(BF16)


def _fox(fq, fk, fv, ccol, crow, b, s, tq):
    nq = s // tq
    qspec = pl.BlockSpec((tq, LANES), lambda bi, hp, qi: (bi * nq + qi, hp))
    kvspec = pl.BlockSpec((s, LANES), lambda bi, hp, qi: (bi, hp))
    return pl.pallas_call(
        functools.partial(_fox_kernel, tq=tq),
        grid=(b, FOX_HEADS // 2, nq),
        in_specs=[qspec, kvspec, kvspec,
                  pl.BlockSpec((tq, LANES), lambda bi, hp, qi: (bi * nq + qi, 0)),
                  pl.BlockSpec((None, 1, s), lambda bi, hp, qi: (bi * FOX_HEADS + 2 * hp, 0, 0)),
                  pl.BlockSpec((None, 1, s), lambda bi, hp, qi: (bi * FOX_HEADS + 2 * hp + 1, 0, 0))],
        out_specs=qspec,
        out_shape=jax.ShapeDtypeStruct(fq.shape, BF16),
        scratch_shapes=[pltpu.VMEM((2, tq, LANES), F32), pltpu.VMEM((2, tq, LANES), F32),
                        pltpu.VMEM((2, tq, LANES), F32)],
        compiler_params=_cparams(("parallel", "parallel", "arbitrary")),
        name="fox",
    )(fq, fk, fv, ccol, crow, crow)


def _swa_kernel(sink_ref, q_ref, kp_ref, kc_ref, vp_ref, vc_ref, o_ref):
    blk = pl.program_id(1)
    lane = lax.broadcasted_iota(jnp.int32, (BLOCK, LANES), 1)
    kb = jnp.concatenate([kp_ref[...], kc_ref[...]], axis=0)
    vb = jnp.concatenate([vp_ref[...], vc_ref[...]], axis=0)
    qi = lax.broadcasted_iota(jnp.int32, (BLOCK, 2 * BLOCK), 0)
    kj = lax.broadcasted_iota(jnp.int32, (BLOCK, 2 * BLOCK), 1)
    dist = qi + BLOCK - kj
    ok = (dist >= 0) & (dist < BLOCK) & ((blk > 0) | (kj >= BLOCK))
    distf = dist.astype(F32)
    rep = SWA_HEADS // SWA_KV_HEADS
    for pair in range(SWA_HEADS // 2):
        q128 = q_ref[:, pair * LANES:(pair + 1) * LANES]
        outs = []
        for half in range(2):
            h = 2 * pair + half
            grp = h // rep
            qa = q128 if half == grp else pltpu.roll(q128, HEAD_D, axis=1)
            in_grp = (lane >= grp * HEAD_D) & (lane < (grp + 1) * HEAD_D)
            qa = jnp.where(in_grp, qa, jnp.zeros_like(qa))
            s = lax.dot_general(qa, kb, (((1,), (1,)), ((), ())), preferred_element_type=F32)
            s = s - (2.0 ** -(h + 1)) * distf
            s = jnp.where(ok, s, NEG_INF)
            sink = sink_ref[h]
            m = jnp.maximum(jnp.max(s, axis=1, keepdims=True), sink)
            p = jnp.exp(s - m)
            den = jnp.sum(p, axis=1, keepdims=True) + jnp.exp(sink - m)
            o = jnp.dot(p.astype(BF16), vb, preferred_element_type=F32) / den
            outs.append(o if half == grp else pltpu.roll(o, HEAD_D, axis=1))
        o_ref[:, pair * LANES:(pair + 1) * LANES] = jnp.where(lane < HEAD_D, outs[0], outs[1]).astype(BF16)


def _swa(sinks, sq, sk, sv, b, s):
    nb = s // BLOCK
    cur = lambda bi, i: (bi * nb + i, 0)
    prev = lambda bi, i: (bi * nb + jnp.maximum(i - 1, 0), 0)
    return pl.pallas_call(
        _swa_kernel,
        grid=(b, nb),
        in_specs=[pl.BlockSpec(memory_space=pltpu.SMEM),
                  pl.BlockSpec((BLOCK, 512), cur),
                  pl.BlockSpec((BLOCK, LANES), prev), pl.BlockSpec((BLOCK, LANES), cur),
                  pl.BlockSpec((BLOCK, LANES), prev), pl.BlockSpec((BLOCK, LANES), cur)],
        out_specs=pl.BlockSpec((BLOCK, 512), cur),
        out_shape=jax.ShapeDtypeStruct(sq.shape, BF16),
        compiler_params=_cparams(("parallel", "parallel")),
        name="swa",
    )(sinks, sq, sk, sk, sv, sv)


def _memkv_kernel(mem_ref, g_ref, w_ref, gk_ref, mk_ref, mv_ref):
    mn = _rms(mem_ref[...], g_ref[...]).astype(BF16)
    seg128 = _seg_matrix(MEM_HD)
    width = MEM_HEADS * MEM_HD
    for hd in range(MEM_HEADS):
        z = jnp.dot(mn, w_ref[:, hd * LANES:(hd + 1) * LANES], preferred_element_type=F32)
        mk_ref[:, hd * LANES:(hd + 1) * LANES] = _seg_norm(z, seg128, gk_ref[...]).astype(BF16)
    mv_ref[...] = jnp.dot(mn, w_ref[:, width:2 * width], preferred_element_type=F32).astype(BF16)


def _memkv(memf, g, w, gk):
    n = memf.shape[0]
    width = MEM_HEADS * MEM_HD
    tm = N_MEM
    return pl.pallas_call(
        _memkv_kernel,
        grid=(n // tm,),
        in_specs=[pl.BlockSpec((tm, D_MODEL), lambda i: (i, 0)), _whole(), _whole(), _whole()],
        out_specs=[pl.BlockSpec((tm, width), lambda i: (i, 0)), pl.BlockSpec((tm, width), lambda i: (i, 0))],
        out_shape=[jax.ShapeDtypeStruct((n, width), BF16), jax.ShapeDtypeStruct((n, width), BF16)],
        compiler_params=_cparams(("parallel",)),
        name="memkv",
    )(memf, g, w, gk)


def _mem_kernel(q_ref, k_ref, v_ref, o_ref):
    for hd in range(MEM_HEADS):
        sl = slice(hd * LANES, (hd + 1) * LANES)
        s = lax.dot_general(q_ref[:, sl], k_ref[:, sl], (((1,), (1,)), ((), ())), preferred_element_type=F32)
        m = jnp.max(s, axis=1, keepdims=True)
        p = jnp.exp(s - m)
        den = jnp.sum(p, axis=1, keepdims=True)
        o_ref[:, sl] = (jnp.dot(p.astype(BF16), v_ref[:, sl], preferred_element_type=F32) / den).astype(BF16)


def _mem(mq, mk, mv, b, s, tq):
    nq = s // tq
    width = MEM_HEADS * MEM_HD
    qspec = pl.BlockSpec((tq, width), lambda bi, qi: (bi * nq + qi, 0))
    kvspec = pl.BlockSpec((N_MEM, width), lambda bi, qi: (bi, 0))
    return pl.pallas_call(
        _mem_kernel,
        grid=(b, nq),
        in_specs=[qspec, kvspec, kvspec],
        out_specs=qspec,
        out_shape=jax.ShapeDtypeStruct(mq.shape, BF16),
        compiler_params=_cparams(("parallel", "parallel")),
        name="mem",
    )(mq, mk, mv)


def _merge_kernel(x_ref, of_ref, os_ref, om_ref, g1_ref, wg_ref, bg_ref, wfo_ref, wso_ref, wmo_ref, wout_ref,
                  g2_ref, wpq_ref, x1_ref, h2_ref, qp_ref):
    x = x_ref[...]
    h = _rms(x, g1_ref[...]).astype(BF16)
    merged = jnp.zeros(x.shape, F32)
    for br, (o_ref, w_ref) in enumerate(((of_ref, wfo_ref), (os_ref, wso_ref), (om_ref, wmo_ref))):
        sl = slice(br * D_MODEL, (br + 1) * D_MODEL)
        gate = _sigmoid(jnp.dot(h, wg_ref[:, sl], preferred_element_type=F32) + bg_ref[:, sl])
        merged = merged + gate * jnp.dot(o_ref[...], w_ref[...], preferred_element_type=F32)
    x1 = x + jnp.dot(merged.astype(BF16), wout_ref[...], preferred_element_type=F32)
    x1_ref[...] = x1
    h2 = _rms(x1, g2_ref[...])
    h2_ref[...] = h2
    qp_ref[...] = jnp.dot(h2.astype(BF16), wpq_ref[...], preferred_element_type=F32).astype(BF16)


def _merge(xf, of, os_, om, g1, wg, bg, wfo, wso, wmo, wout, g2, wpq, tm):
    n = xf.shape[0]
    row = lambda w: pl.BlockSpec((tm, w), lambda i: (i, 0))
    qw = wpq.shape[1]
    return pl.pallas_call(
        _merge_kernel,
        grid=(n // tm,),
        in_specs=[row(D_MODEL), row(512), row(512), row(512)] + [_whole()] * 9,
        out_specs=[row(D_MODEL), row(D_MODEL), row(qw)],
        out_shape=[jax.ShapeDtypeStruct((n, D_MODEL), F32), jax.ShapeDtypeStruct((n, D_MODEL), F32),
                   jax.ShapeDtypeStruct((n, qw), BF16)],
        compiler_params=_cparams(("parallel",)),
        name="merge",
    )(xf, of, os_, om, g1, wg, bg, wfo, wso, wmo, wout, g2, wpq)


def _topk_rows(s, k, order, payload=None):
    big = jnp.float32(2 ** 30)
    vals, sels = [], []
    for _ in range(k):
        m = jnp.max(s, axis=0, keepdims=True)
        sel = jnp.min(jnp.where(s == m, order, big), axis=0, keepdims=True)
        hit = order == sel
        vals.append(m)
        if payload is None:
            sels.append(sel)
        else:
            sels.append(jnp.max(jnp.where(hit, payload, -1.0), axis=0, keepdims=True))
        s = jnp.where(hit, -jnp.inf, s)
    return jnp.concatenate(vals, axis=0), jnp.concatenate(sels, axis=0)


def _pair_candidates(v1, i1, v2, i2):
    t = v1.shape[1]
    sub = lax.broadcasted_iota(jnp.int32, (8, t), 0)
    low = sub < 4
    b_lo = jnp.where(low, sub, sub - 4)
    v2a, i2a = v2[0:8], i2[0:8]
    v2r, i2r = pltpu.roll(v2a, 4, axis=0), pltpu.roll(i2a, 4, axis=0)
    vals, flat, eidx = [], [], []

    def single(a, v2x, i2x, b0, nvalid):
        vals.append(jnp.where(sub < nvalid, v1[a:a + 1] + v2x, -jnp.inf))
        flat.append((a * PEER_TOPK + b0 + sub).astype(F32))
        eidx.append(i1[a:a + 1] * N_KEYS + i2x)

    def double(a, nvalid_a, nvalid_b):
        va = jnp.where(low, v1[a:a + 1], v1[a + 1:a + 2]) + jnp.where(low, v2a, v2r)
        ok = b_lo < jnp.where(low, nvalid_a, nvalid_b)
        vals.append(jnp.where(ok, va, -jnp.inf))
        flat.append((jnp.where(low, a * PEER_TOPK, (a + 1) * PEER_TOPK) + b_lo).astype(F32))
        eidx.append(jnp.where(low, i1[a:a + 1], i1[a + 1:a + 2]) * N_KEYS + jnp.where(low, i2a, i2r))

    single(0, v2a, i2a, 0, 8)
    single(0, v2[8:16], i2[8:16], 8, 8)
    single(1, v2a, i2a, 0, 8)
    single(2, v2a, i2a, 0, 5)
    single(3, v2a, i2a, 0, 4)
    double(4, 3, 2)
    double(6, 2, 2)
    vals.append(v1[8:16] + v2[0:1])
    flat.append(((8 + sub) * PEER_TOPK).astype(F32))
    eidx.append(i1[8:16] * N_KEYS + i2[0:1])
    cat = lambda xs: jnp.concatenate(xs, axis=0)
    return cat(vals), cat(flat), cat(eidx)


def _topk_kernel(qp_ref, k1_ref, k2_ref, e_ref, g_ref):
    nt = (((1,), (1,)), ((), ()))
    rid = lax.broadcasted_iota(jnp.int32, (N_KEYS, qp_ref.shape[0]), 0).astype(F32)
    for hd in range(PEER_HEADS):
        c0 = hd * 2 * N_KEYS
        s1 = lax.dot_general(k1_ref[hd], qp_ref[:, c0:c0 + N_KEYS], nt, preferred_element_type=F32)
        s2 = lax.dot_general(k2_ref[hd], qp_ref[:, c0 + N_KEYS:c0 + 2 * N_KEYS], nt, preferred_element_type=F32)
        v1, i1 = _topk_rows(s1, PEER_TOPK, rid)
        v2, i2 = _topk_rows(s2, PEER_TOPK, rid)
        cand, flat, eidx = _pair_candidates(v1, i1, v2, i2)
        sv, ex = _topk_rows(cand, PEER_TOPK, flat, payload=eidx)
        p = jnp.exp(sv - sv[0:1, :])
        g_ref[hd * PEER_TOPK:(hd + 1) * PEER_TOPK, :] = p / jnp.sum(p, axis=0, keepdims=True)
        e_ref[hd * PEER_TOPK:(hd + 1) * PEER_TOPK, :] = ex.astype(jnp.int32)


def _topk(qp, k1, k2, tb):
    n = qp.shape[0]
    nb = n // tb
    oblk = pl.BlockSpec((None, N_ACT, tb), lambda i: (i, 0, 0))
    return pl.pallas_call(
        _topk_kernel,
        grid=(nb,),
        in_specs=[pl.BlockSpec((tb, qp.shape[1]), lambda i: (i, 0)), _whole(), _whole()],
        out_specs=[oblk, oblk],
        out_shape=[jax.ShapeDtypeStruct((nb, N_ACT, tb), jnp.int32), jax.ShapeDtypeStruct((nb, N_ACT, tb), F32)],
        compiler_params=_cparams(("parallel",)),
        name="topk",
    )(qp, k1, k2)


def _pack_table(w):
    wb = w.astype(BF16)
    half = w.shape[1] // 2
    lo = lax.bitcast_convert_type(wb[:, :half], jnp.uint16).astype(jnp.uint32)
    hi = lax.bitcast_convert_type(wb[:, half:], jnp.uint16).astype(jnp.uint32)
    pk = lax.bitcast_convert_type(lo | (hi << 16), jnp.int32)
    return pk.reshape(w.shape[0] * ROWS_PER_EXPERT, LANES)


GATHER_BUFS = 16
SPLIT_ROWS = 8
NT_DIMS = (((1,), (1,)), ((), ()))


def _group_index_layout(e_tok, grp):
    nb, tb, n_act = e_tok.shape
    seg = n_act // grp
    x = e_tok.reshape(nb, tb // grp, grp, grp, seg)
    return jnp.swapaxes(x, 2, 3).reshape(nb, tb, n_act)


GATHER_CHUNKS = 4
PEER_U_SPAN = 0.85
PEER_V_SPAN = 0.6


def _gather_rows(idx_ref, tab_ref, g_ref, g0, k, grp, chunk=None):
    seg = N_ACT // grp
    rows = [idx_ref.at[g0 + kk] for kk in range(grp)]
    per = seg // GATHER_CHUNKS
    jos = range(seg) if chunk is None else range(chunk * per, (chunk + 1) * per)
    for jo in jos:
        for kk in range(grp):
            j = kk * seg + jo
            e4 = pl.multiple_of(rows[kk][k * seg + jo], ROWS_PER_EXPERT)
            g_ref[ROWS_PER_EXPERT * j:ROWS_PER_EXPERT * (j + 1), :] = tab_ref[pl.ds(e4, ROWS_PER_EXPERT), :]


def _token_pipeline(tb, idx_ref, tab_ref, compute, bufs, compute_span):
    grp = len(bufs) // 2
    sets = (bufs[:grp], bufs[grp:])
    for k in range(grp):
        _gather_rows(idx_ref, tab_ref, sets[0][k], 0, k, grp)

    def phase(src, dst, t0):
        nxt = jnp.minimum(t0 + grp, tb - grp)
        steps = [step for k in range(grp) for step in [compute(t0 + k, src[k])] * ROWS_PER_EXPERT]
        n_gather = grp * GATHER_CHUNKS
        done = 0
        for i in range(n_gather):
            while done < len(steps) and int(done * compute_span * n_gather / len(steps)) <= i:
                next(steps[done], None)
                done += 1
            _gather_rows(idx_ref, tab_ref, dst[i // GATHER_CHUNKS], nxt, i // GATHER_CHUNKS, grp, i % GATHER_CHUNKS)
        for gen in steps[done:]:
            next(gen, None)

    def body(it, c):
        t0 = it * grp
        for parity in range(2):
            @pl.when(it % 2 == parity)
            def _():
                phase(sets[parity], sets[1 - parity], t0)
        return c

    lax.fori_loop(0, tb // grp, body, 0)


def _expert_weights(g_ref, s):
    return pltpu.bitcast(g_ref[pl.ds(s, N_ACT, stride=ROWS_PER_EXPERT), :], BF16)


def _split_rows(ref, tb, rows):
    @pl.when(pl.program_id(0) == 0)
    def _():
        ref[...] = jnp.zeros(ref.shape, ref.dtype)

    for r, val in enumerate(rows):
        for c in range(ref.shape[0]):
            ref.at[c][pl.ds(r, tb, stride=SPLIT_ROWS), :] = val[:, c * LANES:(c + 1) * LANES]


def _merge_rows(ref, tb, r):
    return jnp.concatenate([ref.at[c][pl.ds(r, tb, stride=SPLIT_ROWS), :] for c in range(ref.shape[0])], axis=1)


def _load_tile(ref, t8):
    return jnp.concatenate([ref.at[c][pl.ds(t8, SPLIT_ROWS), :] for c in range(ref.shape[0])], axis=1)


def _store_tile(ref, t8, val):
    for c in range(ref.shape[0]):
        ref.at[c][pl.ds(t8, SPLIT_ROWS), :] = val[:, c * LANES:(c + 1) * LANES]


def _tile_scratch(tb, width):
    return pltpu.VMEM((width // LANES, tb * SPLIT_ROWS, LANES), F32)


def _hi_lo(x):
    hi = x.astype(BF16).astype(F32)
    return hi, x - hi


def _peer_u_body(idx_ref, h_ref, gate_ref, tab_ref, act_ref, *scratch, tb):
    bufs, (r_ref,) = scratch[:GATHER_BUFS], scratch[GATHER_BUFS:]
    half = D_MODEL // 2
    ones = jnp.ones((SPLIT_ROWS, 2 * LANES), BF16)

    def compute(t, g_ref):
        t8 = pl.multiple_of(t * SPLIT_ROWS, SPLIT_ROWS)
        acc = jnp.zeros((N_ACT, LANES), F32)
        hrow = h_ref[pl.ds(t, 1), :]
        for s in range(ROWS_PER_EXPERT):
            w = g_ref[pl.ds(s, N_ACT, stride=ROWS_PER_EXPERT), :]
            lo = pltpu.bitcast(w << 16, F32)
            hi = pltpu.bitcast(w & jnp.int32(-65536), F32)
            acc = (acc + lo * hrow[:, s * LANES:(s + 1) * LANES]
                   + hi * hrow[:, half + s * LANES:half + (s + 1) * LANES])
            if s == ROWS_PER_EXPERT - 1:
                a_hi, a_lo = _hi_lo(acc)
                terms = jnp.concatenate([a_hi.astype(BF16), a_lo.astype(BF16)], axis=1)
                r_ref[0, pl.ds(t8, SPLIT_ROWS), :] = lax.dot_general(ones, terms, NT_DIMS,
                                                                     preferred_element_type=F32)
            yield

    _token_pipeline(tb, idx_ref, tab_ref, compute, bufs, compute_span=PEER_U_SPAN)
    a = _merge_rows(r_ref, tb, 0)
    act_ref[...] = _gelu(a) * jnp.transpose(gate_ref[...])


def _peer_u(idx, h2, gates_t, upk, tb):
    nb = idx.shape[0]
    return pl.pallas_call(
        functools.partial(_peer_u_body, tb=tb),
        grid=(nb,),
        in_specs=[pl.BlockSpec((None, tb, N_ACT), lambda i: (i, 0, 0), memory_space=pltpu.SMEM),
                  pl.BlockSpec((tb, D_MODEL), lambda i: (i, 0)),
                  pl.BlockSpec((None, N_ACT, tb), lambda i: (i, 0, 0)), _whole()],
        out_specs=pl.BlockSpec((None, tb, N_ACT), lambda i: (i, 0, 0)),
        out_shape=jax.ShapeDtypeStruct((nb, tb, N_ACT), F32),
        scratch_shapes=[pltpu.VMEM((N_ACT * ROWS_PER_EXPERT, LANES), jnp.int32)] * GATHER_BUFS
        + [_tile_scratch(tb, N_ACT)],
        compiler_params=_cparams(("arbitrary",)),
        name="peer_u",
    )(idx, h2, gates_t, upk)


def _peer_v_body(idx_ref, act_ref, x1_ref, tab_ref, out_ref, *scratch, tb):
    bufs, (l_ref, o_ref) = scratch[:GATHER_BUFS], scratch[GATHER_BUFS:]
    ahi, alo = _hi_lo(act_ref[...])
    ahi, alo = ahi.astype(BF16), alo.astype(BF16)
    j2 = 2 * lax.broadcasted_iota(jnp.int32, (N_ACT, 2 * N_ACT), 0)
    col = lax.broadcasted_iota(jnp.int32, (N_ACT, 2 * N_ACT), 1)
    spread = [jnp.where(col == j2 + b, 1.0, 0.0).astype(BF16) for b in range(2)]
    _split_rows(l_ref, tb, [jnp.dot(a, e, preferred_element_type=F32) for e in spread for a in (ahi, alo)])

    def compute(t, g_ref):
        t8 = pl.multiple_of(t * SPLIT_ROWS, SPLIT_ROWS)
        lhs = _load_tile(l_ref, t8).astype(BF16)
        for s in range(ROWS_PER_EXPERT):
            o_ref.at[s][pl.ds(t8, SPLIT_ROWS), :] = jnp.dot(lhs, _expert_weights(g_ref, s),
                                                             preferred_element_type=F32)
            yield

    _token_pipeline(tb, idx_ref, tab_ref, compute, bufs, compute_span=PEER_V_SPAN)
    o = [_merge_rows(o_ref, tb, k) for k in range(4)]
    out_ref[...] = x1_ref[...] + jnp.concatenate([o[0] + o[1], o[2] + o[3]], axis=1)


def _peer_v(idx, act, x1, vpk, tb):
    nb = idx.shape[0]
    return pl.pallas_call(
        functools.partial(_peer_v_body, tb=tb),
        grid=(nb,),
        in_specs=[pl.BlockSpec((None, tb, N_ACT), lambda i: (i, 0, 0), memory_space=pltpu.SMEM),
                  pl.BlockSpec((None, tb, N_ACT), lambda i: (i, 0, 0)),
                  pl.BlockSpec((tb, D_MODEL), lambda i: (i, 0)), _whole()],
        out_specs=pl.BlockSpec((tb, D_MODEL), lambda i: (i, 0)),
        out_shape=jax.ShapeDtypeStruct(x1.shape, F32),
        scratch_shapes=[pltpu.VMEM((N_ACT * ROWS_PER_EXPERT, LANES), jnp.int32)] * GATHER_BUFS
        + [_tile_scratch(tb, 2 * N_ACT), _tile_scratch(tb, D_MODEL // 2)],
        compiler_params=_cparams(("arbitrary",)),
        name="peer_v",
    )(idx, act, x1, vpk)


def _tile(n, pref):
    t = min(pref, n)
    while n % t:
        t //= 2
    return t


def kernel(x, mem, norm1_g, w_in, b_gate, b_forget, fox_q_g, fox_k_g, swa_q_g, swa_k_g, swa_sinks, mem_norm_g, w_mem_kv, mem_q_g, mem_k_g, w_fox_o, w_swa_o, w_mem_o, w_out, norm2_g, w_peer_q, peer_keys1, peer_keys2, peer_u, peer_v):
    b, s, d = x.shape
    n = b * s
    assert d == D_MODEL and s % 256 == 0 and mem.shape[1] == N_MEM
    depth = norm1_g.shape[0]
    xf = x.reshape(n, d)
    memf = mem.reshape(b * N_MEM, d)
    tb = BLOCK
    for l in range(depth):
        w = w_in[l]
        o_ff, o_sq, o_sk, o_sv, o_mq, o_gl = 1536, 1544, 2056, 2184, 2312, 2824
        w_att = jnp.concatenate(
            [w[:, 0:o_ff], w[:, o_sq:o_gl], w[:, o_ff:o_sq], jnp.zeros((d, LANES - FOX_HEADS), F32)],
            axis=1).astype(BF16)
        w_gate = w[:, o_gl:].astype(BF16)
        tile2 = lambda g: jnp.concatenate([g, g])
        gains = jnp.stack([tile2(fox_q_g[l]) * HEAD_D ** -0.5, tile2(fox_k_g[l]),
                           tile2(swa_q_g[l]) * HEAD_D ** -0.5, tile2(swa_k_g[l]),
                           mem_q_g[l] * MEM_HD ** -0.5] + [jnp.zeros((LANES,), F32)] * 3)
        bf = jnp.concatenate([b_forget[l], jnp.zeros((LANES - FOX_HEADS,), F32)]).reshape(1, LANES)
        g1 = norm1_g[l].reshape(1, d)

        fq, fk, fv, sq, sk, sv, mq, lf = _inproj(xf, g1, w_att, gains, bf, _tile(n, 512))
        ccol, crow = _cumsum(lf.reshape(b, s, LANES))
        o_fox = _fox(fq, fk, fv, ccol.reshape(n, LANES), crow.reshape(b * FOX_HEADS, 1, s), b, s, 512)
        o_swa = _swa(swa_sinks[l], sq, sk, sv, b, s)
        mk, mv = _memkv(memf, mem_norm_g[l].reshape(1, d), w_mem_kv[l].astype(BF16), mem_k_g[l].reshape(1, LANES))
        o_mem = _mem(mq, mk, mv, b, s, 256)

        x1, h2, qp = _merge(xf, o_fox, o_swa, o_mem, g1, w_gate, b_gate[l].reshape(1, 3 * d),
                            w_fox_o[l].astype(BF16), w_swa_o[l].astype(BF16), w_mem_o[l].astype(BF16),
                            w_out[l].astype(BF16), norm2_g[l].reshape(1, d), w_peer_q[l].astype(BF16), _tile(n, 256))

        e_t, gates_t = _topk(qp, peer_keys1[l].astype(BF16), peer_keys2[l].astype(BF16), tb)
        idx = _group_index_layout(jnp.swapaxes(e_t, 1, 2) * ROWS_PER_EXPERT, GATHER_BUFS // 2)
        act_t = _peer_u(idx, h2, gates_t, _pack_table(peer_u[l]), tb)
        xf = _peer_v(idx, act_t, x1, _pack_table(peer_v[l]), tb)
    return xf.reshape(b, s, d)
```

```python
import functools

import jax
import jax.numpy as jnp
from jax import lax
from jax.experimental import pallas as pl
from jax.experimental.pallas import tpu as pltpu

F32 = jnp.float32
BF16 = jnp.bfloat16

D_MODEL = 1024
BLOCK = 128
FOX_HEADS = 8
HEAD_D = 64
SWA_HEADS = 8
SWA_KV_HEADS = 2
MEM_HEADS = 4
MEM_HD = 128
N_MEM = 256
PEER_HEADS = 8
N_KEYS = 128
N_EXPERTS = N_KEYS * N_KEYS
PEER_TOPK = 16
N_ACT = PEER_HEADS * PEER_TOPK
EPS = 1e-6
NEG_INF = -1e30

LANES = 128
ROWS_PER_EXPERT = 4
VMEM_LIMIT = 48 * 1024 * 1024

C_FQ, C_FK, C_FV, C_SQ, C_SK, C_SV, C_MQ, C_FF = 0, 512, 1024, 1536, 2048, 2176, 2304, 2816
ATT_COLS = 2944


def _cparams(sem):
    return pltpu.CompilerParams(dimension_semantics=sem, vmem_limit_bytes=VMEM_LIMIT)


def _whole():
    return pl.BlockSpec(memory_space=pltpu.VMEM)


def _rms(xf, g):
    return xf * lax.rsqrt(jnp.mean(xf * xf, axis=-1, keepdims=True) + EPS) * g


def _seg_matrix(seg):
    r = (lax.broadcasted_iota(jnp.int32, (2 * LANES, LANES), 0) % LANES) // seg
    c = lax.broadcasted_iota(jnp.int32, (2 * LANES, LANES), 1) // seg
    return jnp.where(r == c, 1.0 / seg, 0.0).astype(BF16)


def _seg_norm(z, seg_mat, gain):
    sq = z * z
    hi = sq.astype(BF16)
    lo = (sq - hi.astype(F32)).astype(BF16)
    ms = jnp.dot(jnp.concatenate([hi, lo], axis=1), seg_mat, preferred_element_type=F32)
    return z * lax.rsqrt(ms + EPS) * gain


def _log_sigmoid(x):
    return jnp.minimum(x, 0.0) - jnp.log1p(jnp.exp(-jnp.abs(x)))


def _sigmoid(x):
    return 1.0 / (1.0 + jnp.exp(-x))


def _gelu(x):
    return 0.5 * x * (1.0 + lax.erf(x * 0.7071067811865476))


def _inproj_kernel(x_ref, g1_ref, w_ref, gains_ref, bf_ref,
                   fq_ref, fk_ref, fv_ref, sq_ref, sk_ref, sv_ref, mq_ref, lf_ref):
    h = _rms(x_ref[...], g1_ref[...]).astype(BF16)
    seg64 = _seg_matrix(HEAD_D)
    seg128 = _seg_matrix(MEM_HD)

    blocks = []
    for out_ref, nblk, seg_mat, gain_row in ((fq_ref, 4, seg64, 0), (fk_ref, 4, seg64, 1), (fv_ref, 4, None, None),
                                            (sq_ref, 4, seg64, 2), (sk_ref, 1, seg64, 3), (sv_ref, 1, None, None),
                                            (mq_ref, 4, seg128, 4)):
        blocks += [(out_ref, b, seg_mat, gain_row) for b in range(nblk)]
    blocks.append((lf_ref, 0, None, None))
    assert len(blocks) * LANES == ATT_COLS

    def proj(c0):
        width = min(2 * LANES, ATT_COLS - c0)
        return jnp.dot(h, w_ref[:, c0:c0 + width], preferred_element_type=F32)

    def finish(z, blk):
        out_ref, b, seg_mat, gain_row = blk
        if out_ref is lf_ref:
            out_ref[...] = _log_sigmoid(z + bf_ref[...])
        elif seg_mat is None:
            out_ref[:, b * LANES:(b + 1) * LANES] = z.astype(BF16)
        else:
            gain = gains_ref[gain_row:gain_row + 1, :]
            out_ref[:, b * LANES:(b + 1) * LANES] = _seg_norm(z, seg_mat, gain).astype(BF16)

    starts = list(range(0, ATT_COLS, 2 * LANES))
    z = proj(starts[0])
    for i, c0 in enumerate(starts):
        z_next = proj(starts[i + 1]) if i + 1 < len(starts) else None
        for half in range(z.shape[1] // LANES):
            finish(z[:, half * LANES:(half + 1) * LANES], blocks[c0 // LANES + half])
        z = z_next


def _inproj(xf, g1, w_att, gains, bf, tm):
    n = xf.shape[0]
    row = lambda w: pl.BlockSpec((tm, w), lambda i: (i, 0))
    out_shape = [jax.ShapeDtypeStruct((n, w), BF16) for w in (512, 512, 512, 512, 128, 128, 512)]
    out_shape.append(jax.ShapeDtypeStruct((n, 128), F32))
    return pl.pallas_call(
        _inproj_kernel,
        grid=(n // tm,),
        in_specs=[row(D_MODEL), _whole(), _whole(), _whole(), _whole()],
        out_specs=[row(512), row(512), row(512), row(512), row(128), row(128), row(512), row(128)],
        out_shape=out_shape,
        compiler_params=_cparams(("parallel",)),
        name="inproj",
    )(xf, g1, w_att, gains, bf)


def _cumsum_kernel(lf_ref, ccol_ref, crow_ref, *, blk):
    s = lf_ref.shape[0]
    r = lax.broadcasted_iota(jnp.int32, (blk, blk), 0)
    c = lax.broadcasted_iota(jnp.int32, (blk, blk), 1)
    tri = jnp.where(c <= r, 1.0, 0.0).astype(F32)
    carry = jnp.zeros((1, LANES), F32)
    for b in range(s // blk):
        x = lf_ref[b * blk:(b + 1) * blk, :]
        cs = jnp.dot(tri, x, preferred_element_type=F32, precision=lax.Precision.HIGHEST) + carry
        ccol_ref[b * blk:(b + 1) * blk, :] = cs
        carry = cs[blk - 1:blk, :]
    ct = jnp.transpose(ccol_ref[...])
    crow_ref[...] = ct[0:FOX_HEADS, :]


def _cumsum(lf3):
    b, s, _ = lf3.shape
    return pl.pallas_call(
        functools.partial(_cumsum_kernel, blk=256),
        grid=(b,),
        in_specs=[pl.BlockSpec((None, s, LANES), lambda i: (i, 0, 0))],
        out_specs=[pl.BlockSpec((None, s, LANES), lambda i: (i, 0, 0)),
                   pl.BlockSpec((None, FOX_HEADS, s), lambda i: (i, 0, 0))],
        out_shape=[jax.ShapeDtypeStruct((b, s, LANES), F32), jax.ShapeDtypeStruct((b, FOX_HEADS, s), F32)],
        compiler_params=_cparams(("parallel",)),
        name="cumsum",
    )(lf3)


def _fox_kernel(q_ref, k_ref, v_ref, ccol_ref, crow0_ref, crow1_ref, o_ref, m_scr, l_scr, acc_scr, *, tq):
    hp = pl.program_id(1)
    qi = pl.program_id(2)
    lane = lax.broadcasted_iota(jnp.int32, (tq, LANES), 1)
    q = q_ref[...]
    zero = jnp.zeros_like(q)
    qh = (jnp.where(lane < HEAD_D, q, zero), jnp.where(lane >= HEAD_D, q, zero))
    cc = ccol_ref[...]
    ci = tuple(jnp.sum(jnp.where(lane == 2 * hp + a, cc, 0.0), axis=1, keepdims=True) for a in range(2))
    crow = (crow0_ref, crow1_ref)
    m_scr[...] = jnp.full(m_scr.shape, NEG_INF, F32)
    l_scr[...] = jnp.zeros(l_scr.shape, F32)
    acc_scr[...] = jnp.zeros(acc_scr.shape, F32)
    rows = lax.broadcasted_iota(jnp.int32, (tq, tq), 0)
    cols = lax.broadcasted_iota(jnp.int32, (tq, tq), 1)

    def step(j, masked):
        ks = pl.multiple_of(j * tq, tq)
        kb = k_ref[pl.ds(ks, tq), :]
        vb = v_ref[pl.ds(ks, tq), :]
        for a in range(2):
            s = lax.dot_general(qh[a], kb, (((1,), (1,)), ((), ())), preferred_element_type=F32)
            s = s + (ci[a] - crow[a][:, pl.ds(ks, tq)])
            if masked:
                s = jnp.where(cols <= rows, s, NEG_INF)
            m_prev = m_scr[a]
            m_next = jnp.maximum(m_prev, jnp.max(s, axis=1, keepdims=True))
            p = jnp.exp(s - jnp.concatenate([m_next] * (tq // LANES), axis=1))
            alpha = jnp.exp(m_prev - m_next)
            l_scr[a] = alpha * l_scr[a] + jnp.sum(p, axis=1, keepdims=True)
            acc_scr[a] = alpha * acc_scr[a] + jnp.dot(p.astype(BF16), vb, preferred_element_type=F32)
            m_scr[a] = m_next

    def body(j, c):
        step(j, False)
        return c

    lax.fori_loop(0, qi, body, 0)
    step(qi, True)
    o0 = acc_scr[0] / l_scr[0]
    o1 = acc_scr[1] / l_scr[1]
    o_ref[...] = jnp.where(lane < HEAD_D, o0, o1).astype(BF16)


def _fox(fq, fk, fv, ccol, crow, b, s, tq):
    nq = s // tq
    qspec = pl.BlockSpec((tq, LANES), lambda bi, hp, qi: (bi * nq + qi, hp))
    kvspec = pl.BlockSpec((s, LANES), lambda bi, hp, qi: (bi, hp))
    return pl.pallas_call(
        functools.partial(_fox_kernel, tq=tq),
        grid=(b, FOX_HEADS // 2, nq),
        in_specs=[qspec, kvspec, kvspec,
                  pl.BlockSpec((tq, LANES), lambda bi, hp, qi: (bi * nq + qi, 0)),
                  pl.BlockSpec((None, 1, s), lambda bi, hp, qi: (bi * FOX_HEADS + 2 * hp, 0, 0)),
                  pl.BlockSpec((None, 1, s), lambda bi, hp, qi: (bi * FOX_HEADS + 2 * hp + 1, 0, 0))],
        out_specs=qspec,
        out_shape=jax.ShapeDtypeStruct(fq.shape, BF16),
        scratch_shapes=[pltpu.VMEM((2, tq, LANES), F32), pltpu.VMEM((2, tq, LANES), F32),
                        pltpu.VMEM((2, tq, LANES), F32)],
        compiler_params=_cparams(("parallel", "parallel", "arbitrary")),
        name="fox",
    )(fq, fk, fv, ccol, crow, crow)


def _swa_kernel(sink_ref, q_ref, kp_ref, kc_ref, vp_ref, vc_ref, o_ref):
    blk = pl.program_id(1)
    lane = lax.broadcasted_iota(jnp.int32, (BLOCK, LANES), 1)
    kb = jnp.concatenate([kp_ref[...], kc_ref[...]], axis=0)
    vb = jnp.concatenate([vp_ref[...], vc_ref[...]], axis=0)
    qi = lax.broadcasted_iota(jnp.int32, (BLOCK, 2 * BLOCK), 0)
    kj = lax.broadcasted_iota(jnp.int32, (BLOCK, 2 * BLOCK), 1)
    dist = qi + BLOCK - kj
    ok = (dist >= 0) & (dist < BLOCK) & ((blk > 0) | (kj >= BLOCK))
    distf = dist.astype(F32)
    rep = SWA_HEADS // SWA_KV_HEADS
    for pair in range(SWA_HEADS // 2):
        q128 = q_ref[:, pair * LANES:(pair + 1) * LANES]
        outs = []
        for half in range(2):
            h = 2 * pair + half
            grp = h // rep
            qa = q128 if half == grp else pltpu.roll(q128, HEAD_D, axis=1)
            in_grp = (lane >= grp * HEAD_D) & (lane < (grp + 1) * HEAD_D)
            qa = jnp.where(in_grp, qa, jnp.zeros_like(qa))
            s = lax.dot_general(qa, kb, (((1,), (1,)), ((), ())), preferred_element_type=F32)
            s = s - (2.0 ** -(h + 1)) * distf
            s = jnp.where(ok, s, NEG_INF)
            sink = sink_ref[h]
            m = jnp.maximum(jnp.max(s, axis=1, keepdims=True), sink)
            p = jnp.exp(s - m)
            den = jnp.sum(p, axis=1, keepdims=True) + jnp.exp(sink - m)
            o = jnp.dot(p.astype(BF16), vb, preferred_element_type=F32) / den
            outs.append(o if half == grp else pltpu.roll(o, HEAD_D, axis=1))
        o_ref[:, pair * LANES:(pair + 1) * LANES] = jnp.where(lane < HEAD_D, outs[0], outs[1]).astype(BF16)


def _swa(sinks, sq, sk, sv, b, s):
    nb = s // BLOCK
    cur = lambda bi, i: (bi * nb + i, 0)
    prev = lambda bi, i: (bi * nb + jnp.maximum(i - 1, 0), 0)
    return pl.pallas_call(
        _swa_kernel,
        grid=(b, nb),
        in_specs=[pl.BlockSpec(memory_space=pltpu.SMEM),
                  pl.BlockSpec((BLOCK, 512), cur),
                  pl.BlockSpec((BLOCK, LANES), prev), pl.BlockSpec((BLOCK, LANES), cur),
                  pl.BlockSpec((BLOCK, LANES), prev), pl.BlockSpec((BLOCK, LANES), cur)],
        out_specs=pl.BlockSpec((BLOCK, 512), cur),
        out_shape=jax.ShapeDtypeStruct(sq.shape, BF16),
        compiler_params=_cparams(("parallel", "parallel")),
        name="swa",
    )(sinks, sq, sk, sk, sv, sv)


def _memkv_kernel(mem_ref, g_ref, w_ref, gk_ref, mk_ref, mv_ref):
    mn = _rms(mem_ref[...], g_ref[...]).astype(BF16)
    seg128 = _seg_matrix(MEM_HD)
    width = MEM_HEADS * MEM_HD
    for hd in range(MEM_HEADS):
        z = jnp.dot(mn, w_ref[:, hd * LANES:(hd + 1) * LANES], preferred_element_type=F32)
        mk_ref[:, hd * LANES:(hd + 1) * LANES] = _seg_norm(z, seg128, gk_ref[...]).astype(BF16)
    mv_ref[...] = jnp.dot(mn, w_ref[:, width:2 * width], preferred_element_type=F32).astype(BF16)


def _memkv(memf, g, w, gk):
    n = memf.shape[0]
    width = MEM_HEADS * MEM_HD
    tm = N_MEM
    return pl.pallas_call(
        _memkv_kernel,
        grid=(n // tm,),
        in_specs=[pl.BlockSpec((tm, D_MODEL), lambda i: (i, 0)), _whole(), _whole(), _whole()],
        out_specs=[pl.BlockSpec((tm, width), lambda i: (i, 0)), pl.BlockSpec((tm, width), lambda i: (i, 0))],
        out_shape=[jax.ShapeDtypeStruct((n, width), BF16), jax.ShapeDtypeStruct((n, width), BF16)],
        compiler_params=_cparams(("parallel",)),
        name="memkv",
    )(memf, g, w, gk)


def _mem_kernel(q_ref, k_ref, v_ref, o_ref):
    for hd in range(MEM_HEADS):
        sl = slice(hd * LANES, (hd + 1) * LANES)
        s = lax.dot_general(q_ref[:, sl], k_ref[:, sl], (((1,), (1,)), ((), ())), preferred_element_type=F32)
        m = jnp.max(s, axis=1, keepdims=True)
        p = jnp.exp(s - m)
        den = jnp.sum(p, axis=1, keepdims=True)
        o_ref[:, sl] = (jnp.dot(p.astype(BF16), v_ref[:, sl], preferred_element_type=F32) / den).astype(BF16)


def _mem(mq, mk, mv, b, s, tq):
    nq = s // tq
    width = MEM_HEADS * MEM_HD
    qspec = pl.BlockSpec((tq, width), lambda bi, qi: (bi * nq + qi, 0))
    kvspec = pl.BlockSpec((N_MEM, width), lambda bi, qi: (bi, 0))
    return pl.pallas_call(
        _mem_kernel,
        grid=(b, nq),
        in_specs=[qspec, kvspec, kvspec],
        out_specs=qspec,
        out_shape=jax.ShapeDtypeStruct(mq.shape, BF16),
        compiler_params=_cparams(("parallel", "parallel")),
        name="mem",
    )(mq, mk, mv)


def _merge_kernel(x_ref, of_ref, os_ref, om_ref, g1_ref, wg_ref, bg_ref, wfo_ref, wso_ref, wmo_ref, wout_ref,
                  g2_ref, wpq_ref, x1_ref, h2_ref, qp_ref):
    x = x_ref[...]
    h = _rms(x, g1_ref[...]).astype(BF16)
    merged = jnp.zeros(x.shape, F32)
    for br, (o_ref, w_ref) in enumerate(((of_ref, wfo_ref), (os_ref, wso_ref), (om_ref, wmo_ref))):
        sl = slice(br * D_MODEL, (br + 1) * D_MODEL)
        gate = _sigmoid(jnp.dot(h, wg_ref[:, sl], preferred_element_type=F32) + bg_ref[:, sl])
        merged = merged + gate * jnp.dot(o_ref[...], w_ref[...], preferred_element_type=F32)
    x1 = x + jnp.dot(merged.astype(BF16), wout_ref[...], preferred_element_type=F32)
    x1_ref[...] = x1
    h2 = _rms(x1, g2_ref[...])
    h2_ref[...] = h2
    qp_ref[...] = jnp.dot(h2.astype(BF16), wpq_ref[...], preferred_element_type=F32).astype(BF16)


def _merge(xf, of, os_, om, g1, wg, bg, wfo, wso, wmo, wout, g2, wpq, tm):
    n = xf.shape[0]
    row = lambda w: pl.BlockSpec((tm, w), lambda i: (i, 0))
    qw = wpq.shape[1]
    return pl.pallas_call(
        _merge_kernel,
        grid=(n // tm,),
        in_specs=[row(D_MODEL), row(512), row(512), row(512)] + [_whole()] * 9,
        out_specs=[row(D_MODEL), row(D_MODEL), row(qw)],
        out_shape=[jax.ShapeDtypeStruct((n, D_MODEL), F32), jax.ShapeDtypeStruct((n, D_MODEL), F32),
                   jax.ShapeDtypeStruct((n, qw), BF16)],
        compiler_params=_cparams(("parallel",)),
        name="merge",
    )(xf, of, os_, om, g1, wg, bg, wfo, wso, wmo, wout, g2, wpq)


def _topk_rows(s, k, order, payload=None):
    big = jnp.float32(2 ** 30)
    vals, sels = [], []
    for _ in range(k):
        m = jnp.max(s, axis=0, keepdims=True)
        sel = jnp.min(jnp.where(s == m, order, big), axis=0, keepdims=True)
        hit = order == sel
        vals.append(m)
        if payload is None:
            sels.append(sel)
        else:
            sels.append(jnp.max(jnp.where(hit, payload, -1.0), axis=0, keepdims=True))
        s = jnp.where(hit, -jnp.inf, s)
    return jnp.concatenate(vals, axis=0), jnp.concatenate(sels, axis=0)


def _topk_keys(s, k):
    depth = 4
    width = s.shape[0] // depth
    rid = lax.broadcasted_iota(jnp.int32, (width, s.shape[1]), 0).astype(F32)
    val = [s[i * width:(i + 1) * width] for i in range(depth)]
    idx = [rid + float(i * width) for i in range(depth)]

    def exchange(a, b, index_ordered):
        swap = val[b] > val[a]
        if not index_ordered:
            swap = swap | ((val[b] == val[a]) & (idx[b] < idx[a]))
        val[a], val[b] = jnp.where(swap, val[b], val[a]), jnp.where(swap, val[a], val[b])
        idx[a], idx[b] = jnp.where(swap, idx[b], idx[a]), jnp.where(swap, idx[a], idx[b])

    exchange(0, 1, True)
    exchange(2, 3, True)
    exchange(0, 2, True)
    exchange(1, 3, True)
    exchange(1, 2, False)

    big = jnp.float32(2 ** 30)
    vals, sels = [], []
    for _ in range(k):
        m = jnp.max(val[0], axis=0, keepdims=True)
        sel = jnp.min(jnp.where(val[0] == m, idx[0], big), axis=0, keepdims=True)
        hit = idx[0] == sel
        vals.append(m)
        sels.append(sel)
        for d in range(depth - 1):
            val[d] = jnp.where(hit, val[d + 1], val[d])
            idx[d] = jnp.where(hit, idx[d + 1], idx[d])
        val[depth - 1] = jnp.where(hit, -jnp.inf, val[depth - 1])
    return jnp.concatenate(vals, axis=0), jnp.concatenate(sels, axis=0)


def _pair_candidates(v1, i1, v2, i2):
    t = v1.shape[1]
    sub = lax.broadcasted_iota(jnp.int32, (8, t), 0)
    low = sub < 4
    b_lo = jnp.where(low, sub, sub - 4)
    v2a, i2a = v2[0:8], i2[0:8]
    v2r, i2r = pltpu.roll(v2a, 4, axis=0), pltpu.roll(i2a, 4, axis=0)
    vals, flat, eidx = [], [], []

    def single(a, v2x, i2x, b0, nvalid):
        vals.append(jnp.where(sub < nvalid, v1[a:a + 1] + v2x, -jnp.inf))
        flat.append((a * PEER_TOPK + b0 + sub).astype(F32))
        eidx.append(i1[a:a + 1] * N_KEYS + i2x)

    def double(a, nvalid_a, nvalid_b):
        va = jnp.where(low, v1[a:a + 1], v1[a + 1:a + 2]) + jnp.where(low, v2a, v2r)
        ok = b_lo < jnp.where(low, nvalid_a, nvalid_b)
        vals.append(jnp.where(ok, va, -jnp.inf))
        flat.append((jnp.where(low, a * PEER_TOPK, (a + 1) * PEER_TOPK) + b_lo).astype(F32))
        eidx.append(jnp.where(low, i1[a:a + 1], i1[a + 1:a + 2]) * N_KEYS + jnp.where(low, i2a, i2r))

    single(0, v2a, i2a, 0, 8)
    single(0, v2[8:16], i2[8:16], 8, 8)
    single(1, v2a, i2a, 0, 8)
    single(2, v2a, i2a, 0, 5)
    single(3, v2a, i2a, 0, 4)
    double(4, 3, 2)
    double(6, 2, 2)
    vals.append(v1[8:16] + v2[0:1])
    flat.append(((8 + sub) * PEER_TOPK).astype(F32))
    eidx.append(i1[8:16] * N_KEYS + i2[0:1])
    cat = lambda xs: jnp.concatenate(xs, axis=0)
    return cat(vals), cat(flat), cat(eidx)


def _topk_kernel(qp_ref, k1_ref, k2_ref, e_ref, g_ref):
    nt = (((1,), (1,)), ((), ()))
    for hd in range(PEER_HEADS):
        c0 = hd * 2 * N_KEYS
        s1 = lax.dot_general(k1_ref[hd], qp_ref[:, c0:c0 + N_KEYS], nt, preferred_element_type=F32)
        s2 = lax.dot_general(k2_ref[hd], qp_ref[:, c0 + N_KEYS:c0 + 2 * N_KEYS], nt, preferred_element_type=F32)
        v1, i1 = _topk_keys(s1, PEER_TOPK)
        v2, i2 = _topk_keys(s2, PEER_TOPK)
        cand, flat, eidx = _pair_candidates(v1, i1, v2, i2)
        sv, ex = _topk_rows(cand, PEER_TOPK, flat, payload=eidx)
        p = jnp.exp(sv - sv[0:1, :])
        g_ref[hd * PEER_TOPK:(hd + 1) * PEER_TOPK, :] = p / jnp.sum(p, axis=0, keepdims=True)
        e_ref[hd * PEER_TOPK:(hd + 1) * PEER_TOPK, :] = ex.astype(jnp.int32)


def _topk(qp, k1, k2, tb):
    n = qp.shape[0]
    nb = n // tb
    oblk = pl.BlockSpec((None, N_ACT, tb), lambda i: (i, 0, 0))
    return pl.pallas_call(
        _topk_kernel,
        grid=(nb,),
        in_specs=[pl.BlockSpec((tb, qp.shape[1]), lambda i: (i, 0)), _whole(), _whole()],
        out_specs=[oblk, oblk],
        out_shape=[jax.ShapeDtypeStruct((nb, N_ACT, tb), jnp.int32), jax.ShapeDtypeStruct((nb, N_ACT, tb), F32)],
        compiler_params=_cparams(("parallel",)),
        name="topk",
    )(qp, k1, k2)


def _pack_table(w):
    wb = w.astype(BF16)
    half = w.shape[1] // 2
    lo = lax.bitcast_convert_type(wb[:, :half], jnp.uint16).astype(jnp.uint32)
    hi = lax.bitcast_convert_type(wb[:, half:], jnp.uint16).astype(jnp.uint32)
    pk = lax.bitcast_convert_type(lo | (hi << 16), jnp.int32)
    return pk.reshape(w.shape[0] * ROWS_PER_EXPERT, LANES)


GATHER_BUFS = 16
SPLIT_ROWS = 8
NT_DIMS = (((1,), (1,)), ((), ()))


def _group_index_layout(e_tok, grp):
    nb, tb, n_act = e_tok.shape
    seg = n_act // grp
    x = e_tok.reshape(nb, tb // grp, grp, grp, seg)
    return jnp.swapaxes(x, 2, 3).reshape(nb, tb, n_act)


GATHER_CHUNKS = 4
PEER_U_SPAN = 0.85
PEER_V_SPAN = 0.6


def _gather_rows(idx_ref, tab_ref, g_ref, g0, k, grp, chunk=None):
    seg = N_ACT // grp
    rows = [idx_ref.at[g0 + kk] for kk in range(grp)]
    per = seg // GATHER_CHUNKS
    jos = range(seg) if chunk is None else range(chunk * per, (chunk + 1) * per)
    for jo in jos:
        for kk in range(grp):
            j = kk * seg + jo
            e4 = pl.multiple_of(rows[kk][k * seg + jo], ROWS_PER_EXPERT)
            g_ref[ROWS_PER_EXPERT * j:ROWS_PER_EXPERT * (j + 1), :] = tab_ref[pl.ds(e4, ROWS_PER_EXPERT), :]


def _token_pipeline(tb, idx_ref, tab_ref, compute, bufs, compute_span):
    grp = len(bufs) // 2
    sets = (bufs[:grp], bufs[grp:])
    for k in range(grp):
        _gather_rows(idx_ref, tab_ref, sets[0][k], 0, k, grp)

    def phase(src, dst, t0):
        nxt = jnp.minimum(t0 + grp, tb - grp)
        steps = [step for k in range(grp) for step in [compute(t0 + k, src[k])] * ROWS_PER_EXPERT]
        n_gather = grp * GATHER_CHUNKS
        done = 0
        for i in range(n_gather):
            while done < len(steps) and int(done * compute_span * n_gather / len(steps)) <= i:
                next(steps[done], None)
                done += 1
            _gather_rows(idx_ref, tab_ref, dst[i // GATHER_CHUNKS], nxt, i // GATHER_CHUNKS, grp, i % GATHER_CHUNKS)
        for gen in steps[done:]:
            next(gen, None)

    def body(it, c):
        t0 = it * grp
        for parity in range(2):
            @pl.when(it % 2 == parity)
            def _():
                phase(sets[parity], sets[1 - parity], t0)
        return c

    lax.fori_loop(0, tb // grp, body, 0)


def _expert_weights(g_ref, s):
    return pltpu.bitcast(g_ref[pl.ds(s, N_ACT, stride=ROWS_PER_EXPERT), :], BF16)


def _split_rows(ref, tb, rows):
    @pl.when(pl.program_id(0) == 0)
    def _():
        ref[...] = jnp.zeros(ref.shape, ref.dtype)

    for r, val in enumerate(rows):
        for c in range(ref.shape[0]):
            ref.at[c][pl.ds(r, tb, stride=SPLIT_ROWS), :] = val[:, c * LANES:(c + 1) * LANES]


def _merge_rows(ref, tb, r):
    return jnp.concatenate([ref.at[c][pl.ds(r, tb, stride=SPLIT_ROWS), :] for c in range(ref.shape[0])], axis=1)


def _load_tile(ref, t8):
    return jnp.concatenate([ref.at[c][pl.ds(t8, SPLIT_ROWS), :] for c in range(ref.shape[0])], axis=1)


def _store_tile(ref, t8, val):
    for c in range(ref.shape[0]):
        ref.at[c][pl.ds(t8, SPLIT_ROWS), :] = val[:, c * LANES:(c + 1) * LANES]


def _tile_scratch(tb, width):
    return pltpu.VMEM((width // LANES, tb * SPLIT_ROWS, LANES), F32)


def _hi_lo(x):
    hi = x.astype(BF16).astype(F32)
    return hi, x - hi


def _peer_u_body(idx_ref, h_ref, gate_ref, tab_ref, act_ref, *scratch, tb):
    bufs, (r_ref,) = scratch[:GATHER_BUFS], scratch[GATHER_BUFS:]
    half = D_MODEL // 2
    ones = jnp.ones((SPLIT_ROWS, 2 * LANES), BF16)

    def compute(t, g_ref):
        t8 = pl.multiple_of(t * SPLIT_ROWS, SPLIT_ROWS)
        acc = jnp.zeros((N_ACT, LANES), F32)
        hrow = h_ref[pl.ds(t, 1), :]
        for s in range(ROWS_PER_EXPERT):
            w = g_ref[pl.ds(s, N_ACT, stride=ROWS_PER_EXPERT), :]
            lo = pltpu.bitcast(w << 16, F32)
            hi = pltpu.bitcast(w & jnp.int32(-65536), F32)
            acc = (acc + lo * hrow[:, s * LANES:(s + 1) * LANES]
                   + hi * hrow[:, half + s * LANES:half + (s + 1) * LANES])
            if s == ROWS_PER_EXPERT - 1:
                a_hi, a_lo = _hi_lo(acc)
                terms = jnp.concatenate([a_hi.astype(BF16), a_lo.astype(BF16)], axis=1)
                r_ref[0, pl.ds(t8, SPLIT_ROWS), :] = lax.dot_general(ones, terms, NT_DIMS,
                                                                     preferred_element_type=F32)
            yield

    _token_pipeline(tb, idx_ref, tab_ref, compute, bufs, compute_span=PEER_U_SPAN)
    a = _merge_rows(r_ref, tb, 0)
    act_ref[...] = _gelu(a) * jnp.transpose(gate_ref[...])


def _peer_u(idx, h2, gates_t, upk, tb):
    nb = idx.shape[0]
    return pl.pallas_call(
        functools.partial(_peer_u_body, tb=tb),
        grid=(nb,),
        in_specs=[pl.BlockSpec((None, tb, N_ACT), lambda i: (i, 0, 0), memory_space=pltpu.SMEM),
                  pl.BlockSpec((tb, D_MODEL), lambda i: (i, 0)),
                  pl.BlockSpec((None, N_ACT, tb), lambda i: (i, 0, 0)), _whole()],
        out_specs=pl.BlockSpec((None, tb, N_ACT), lambda i: (i, 0, 0)),
        out_shape=jax.ShapeDtypeStruct((nb, tb, N_ACT), F32),
        scratch_shapes=[pltpu.VMEM((N_ACT * ROWS_PER_EXPERT, LANES), jnp.int32)] * GATHER_BUFS
        + [_tile_scratch(tb, N_ACT)],
        compiler_params=_cparams(("arbitrary",)),
        name="peer_u",
    )(idx, h2, gates_t, upk)


def _peer_v_body(idx_ref, act_ref, x1_ref, tab_ref, out_ref, *scratch, tb):
    bufs, (l_ref, o_ref) = scratch[:GATHER_BUFS], scratch[GATHER_BUFS:]
    ahi, alo = _hi_lo(act_ref[...])
    ahi, alo = ahi.astype(BF16), alo.astype(BF16)
    j2 = 2 * lax.broadcasted_iota(jnp.int32, (N_ACT, 2 * N_ACT), 0)
    col = lax.broadcasted_iota(jnp.int32, (N_ACT, 2 * N_ACT), 1)
    spread = [jnp.where(col == j2 + b, 1.0, 0.0).astype(BF16) for b in range(2)]
    _split_rows(l_ref, tb, [jnp.dot(a, e, preferred_element_type=F32) for e in spread for a in (ahi, alo)])

    def compute(t, g_ref):
        t8 = pl.multiple_of(t * SPLIT_ROWS, SPLIT_ROWS)
        lhs = _load_tile(l_ref, t8).astype(BF16)
        for s in range(ROWS_PER_EXPERT):
            o_ref.at[s][pl.ds(t8, SPLIT_ROWS), :] = jnp.dot(lhs, _expert_weights(g_ref, s),
                                                             preferred_element_type=F32)
            yield

    _token_pipeline(tb, idx_ref, tab_ref, compute, bufs, compute_span=PEER_V_SPAN)
    o = [_merge_rows(o_ref, tb, k) for k in range(4)]
    out_ref[...] = x1_ref[...] + jnp.concatenate([o[0] + o[1], o[2] + o[3]], axis=1)


def _peer_v(idx, act, x1, vpk, tb):
    nb = idx.shape[0]
    return pl.pallas_call(
        functools.partial(_peer_v_body, tb=tb),
        grid=(nb,),
        in_specs=[pl.BlockSpec((None, tb, N_ACT), lambda i: (i, 0, 0), memory_space=pltpu.SMEM),
                  pl.BlockSpec((None, tb, N_ACT), lambda i: (i, 0, 0)),
                  pl.BlockSpec((tb, D_MODEL), lambda i: (i, 0)), _whole()],
        out_specs=pl.BlockSpec((tb, D_MODEL), lambda i: (i, 0)),
        out_shape=jax.ShapeDtypeStruct(x1.shape, F32),
        scratch_shapes=[pltpu.VMEM((N_ACT * ROWS_PER_EXPERT, LANES), jnp.int32)] * GATHER_BUFS
        + [_tile_scratch(tb, 2 * N_ACT), _tile_scratch(tb, D_MODEL // 2)],
        compiler_params=_cparams(("arbitrary",)),
        name="peer_v",
    )(idx, act, x1, vpk)


def _tile(n, pref):
    t = min(pref, n)
    while n % t:
        t //= 2
    return t


def kernel(x, mem, norm1_g, w_in, b_gate, b_forget, fox_q_g, fox_k_g, swa_q_g, swa_k_g, swa_sinks, mem_norm_g, w_mem_kv, mem_q_g, mem_k_g, w_fox_o, w_swa_o, w_mem_o, w_out, norm2_g, w_peer_q, peer_keys1, peer_keys2, peer_u, peer_v):
    b, s, d = x.shape
    n = b * s
    assert d == D_MODEL and s % 256 == 0 and mem.shape[1] == N_MEM
    depth = norm1_g.shape[0]
    xf = x.reshape(n, d)
    memf = mem.reshape(b * N_MEM, d)
    tb = BLOCK
    for l in range(depth):
        w = w_in[l]
        o_ff, o_sq, o_sk, o_sv, o_mq, o_gl = 1536, 1544, 2056, 2184, 2312, 2824
        w_att = jnp.concatenate(
            [w[:, 0:o_ff], w[:, o_sq:o_gl], w[:, o_ff:o_sq], jnp.zeros((d, LANES - FOX_HEADS), F32)],
            axis=1).astype(BF16)
        w_gate = w[:, o_gl:].astype(BF16)
        tile2 = lambda g: jnp.concatenate([g, g])
        gains = jnp.stack([tile2(fox_q_g[l]) * HEAD_D ** -0.5, tile2(fox_k_g[l]),
                           tile2(swa_q_g[l]) * HEAD_D ** -0.5, tile2(swa_k_g[l]),
                           mem_q_g[l] * MEM_HD ** -0.5] + [jnp.zeros((LANES,), F32)] * 3)
        bf = jnp.concatenate([b_forget[l], jnp.zeros((LANES - FOX_HEADS,), F32)]).reshape(1, LANES)
        g1 = norm1_g[l].reshape(1, d)

        fq, fk, fv, sq, sk, sv, mq, lf = _inproj(xf, g1, w_att, gains, bf, _tile(n, 512))
        ccol, crow = _cumsum(lf.reshape(b, s, LANES))
        o_fox = _fox(fq, fk, fv, ccol.reshape(n, LANES), crow.reshape(b * FOX_HEADS, 1, s), b, s, 512)
        o_swa = _swa(swa_sinks[l], sq, sk, sv, b, s)
        mk, mv = _memkv(memf, mem_norm_g[l].reshape(1, d), w_mem_kv[l].astype(BF16), mem_k_g[l].reshape(1, LANES))
        o_mem = _mem(mq, mk, mv, b, s, 256)

        x1, h2, qp = _merge(xf, o_fox, o_swa, o_mem, g1, w_gate, b_gate[l].reshape(1, 3 * d),
                            w_fox_o[l].astype(BF16), w_swa_o[l].astype(BF16), w_mem_o[l].astype(BF16),
                            w_out[l].astype(BF16), norm2_g[l].reshape(1, d), w_peer_q[l].astype(BF16), _tile(n, 256))

        e_t, gates_t = _topk(qp, peer_keys1[l].astype(BF16), peer_keys2[l].astype(BF16), tb)
        idx = _group_index_layout(jnp.swapaxes(e_t, 1, 2) * ROWS_PER_EXPERT, GATHER_BUFS // 2)
        act_t = _peer_u(idx, h2, gates_t, _pack_table(peer_u[l]), tb)
        xf = _peer_v(idx, act_t, x1, _pack_table(peer_v[l]), tb)
    return xf.reshape(b, s, d)
```

```python
import functools

import jax
import jax.numpy as jnp
from jax import lax
from jax.experimental import pallas as pl
from jax.experimental.pallas import tpu as pltpu

F32 = jnp.float32
BF16 = jnp.bfloat16

D_MODEL = 1024
BLOCK = 128
FOX_HEADS = 8
HEAD_D = 64
SWA_HEADS = 8
SWA_KV_HEADS = 2
MEM_HEADS = 4
MEM_HD = 128
N_MEM = 256
PEER_HEADS = 8
N_KEYS = 128
N_EXPERTS = N_KEYS * N_KEYS
PEER_TOPK = 16
N_ACT = PEER_HEADS * PEER_TOPK
EPS = 1e-6
NEG_INF = -1e30

LANES = 128
ROWS_PER_EXPERT = 4
VMEM_LIMIT = 48 * 1024 * 1024

C_FQ, C_FK, C_FV, C_SQ, C_SK, C_SV, C_MQ, C_FF = 0, 512, 1024, 1536, 2048, 2176, 2304, 2816
ATT_COLS = 2944


def _cparams(sem):
    return pltpu.CompilerParams(dimension_semantics=sem, vmem_limit_bytes=VMEM_LIMIT)


def _whole():
    return pl.BlockSpec(memory_space=pltpu.VMEM)


def _rms(xf, g):
    return xf * lax.rsqrt(jnp.mean(xf * xf, axis=-1, keepdims=True) + EPS) * g


def _seg_matrix(seg):
    r = (lax.broadcasted_iota(jnp.int32, (2 * LANES, LANES), 0) % LANES) // seg
    c = lax.broadcasted_iota(jnp.int32, (2 * LANES, LANES), 1) // seg
    return jnp.where(r == c, 1.0 / seg, 0.0).astype(BF16)


def _seg_norm(z, seg_mat, gain):
    sq = z * z
    hi = sq.astype(BF16)
    lo = (sq - hi.astype(F32)).astype(BF16)
    ms = jnp.dot(jnp.concatenate([hi, lo], axis=1), seg_mat, preferred_element_type=F32)
    return z * lax.rsqrt(ms + EPS) * gain


def _log_sigmoid(x):
    return jnp.minimum(x, 0.0) - jnp.log1p(jnp.exp(-jnp.abs(x)))


def _sigmoid(x):
    return 1.0 / (1.0 + jnp.exp(-x))


def _gelu(x):
    return 0.5 * x * (1.0 + lax.erf(x * 0.7071067811865476))


def _inproj_kernel(x_ref, g1_ref, w_ref, gains_ref, bf_ref,
                   fq_ref, fk_ref, fv_ref, sq_ref, sk_ref, sv_ref, mq_ref, lf_ref):
    h = _rms(x_ref[...], g1_ref[...]).astype(BF16)
    seg64 = _seg_matrix(HEAD_D)
    seg128 = _seg_matrix(MEM_HD)

    blocks = []
    for out_ref, nblk, seg_mat, gain_row in ((fq_ref, 4, seg64, 0), (fk_ref, 4, seg64, 1), (fv_ref, 4, None, None),
                                            (sq_ref, 4, seg64, 2), (sk_ref, 1, seg64, 3), (sv_ref, 1, None, None),
                                            (mq_ref, 4, seg128, 4)):
        blocks += [(out_ref, b, seg_mat, gain_row) for b in range(nblk)]
    blocks.append((lf_ref, 0, None, None))
    assert len(blocks) * LANES == ATT_COLS

    def proj(c0):
        width = min(2 * LANES, ATT_COLS - c0)
        return jnp.dot(h, w_ref[:, c0:c0 + width], preferred_element_type=F32)

    def finish(z, blk):
        out_ref, b, seg_mat, gain_row = blk
        if out_ref is lf_ref:
            out_ref[...] = _log_sigmoid(z + bf_ref[...])
        elif seg_mat is None:
            out_ref[:, b * LANES:(b + 1) * LANES] = z.astype(BF16)
        else:
            gain = gains_ref[gain_row:gain_row + 1, :]
            out_ref[:, b * LANES:(b + 1) * LANES] = _seg_norm(z, seg_mat, gain).astype(BF16)

    starts = list(range(0, ATT_COLS, 2 * LANES))
    z = proj(starts[0])
    for i, c0 in enumerate(starts):
        z_next = proj(starts[i + 1]) if i + 1 < len(starts) else None
        for half in range(z.shape[1] // LANES):
            finish(z[:, half * LANES:(half + 1) * LANES], blocks[c0 // LANES + half])
        z = z_next


def _inproj(xf, g1, w_att, gains, bf, tm):
    n = xf.shape[0]
    row = lambda w: pl.BlockSpec((tm, w), lambda i: (i, 0))
    out_shape = [jax.ShapeDtypeStruct((n, w), BF16) for w in (512, 512, 512, 512, 128, 128, 512)]
    out_shape.append(jax.ShapeDtypeStruct((n, 128), F32))
    return pl.pallas_call(
        _inproj_kernel,
        grid=(n // tm,),
        in_specs=[row(D_MODEL), _whole(), _whole(), _whole(), _whole()],
        out_specs=[row(512), row(512), row(512), row(512), row(128), row(128), row(512), row(128)],
        out_shape=out_shape,
        compiler_params=_cparams(("parallel",)),
        name="inproj",
    )(xf, g1, w_att, gains, bf)


def _cumsum_kernel(lf_ref, crow_ref, ccol_ref, *, blk):
    s = lf_ref.shape[0]
    r = lax.broadcasted_iota(jnp.int32, (blk, blk), 0)
    c = lax.broadcasted_iota(jnp.int32, (blk, blk), 1)
    tri = jnp.where(c <= r, 1.0, 0.0).astype(F32)
    carry = jnp.zeros((1, LANES), F32)
    for b in range(s // blk):
        x = lf_ref[b * blk:(b + 1) * blk, :]
        cs = jnp.dot(tri, x, preferred_element_type=F32, precision=lax.Precision.HIGHEST) + carry
        ccol_ref[b * blk:(b + 1) * blk, :] = cs
        carry = cs[blk - 1:blk, :]
    ct = jnp.transpose(ccol_ref[...])
    crow_ref[...] = ct[0:FOX_HEADS, :]


def _cumsum(lf3):
    b, s, _ = lf3.shape
    return pl.pallas_call(
        functools.partial(_cumsum_kernel, blk=256),
        grid=(b,),
        in_specs=[pl.BlockSpec((None, s, LANES), lambda i: (i, 0, 0))],
        out_specs=pl.BlockSpec((None, FOX_HEADS, s), lambda i: (i, 0, 0)),
        out_shape=jax.ShapeDtypeStruct((b, FOX_HEADS, s), F32),
        scratch_shapes=[pltpu.VMEM((s, LANES), F32)],
        compiler_params=_cparams(("parallel",)),
        name="cumsum",
    )(lf3)


def _fox_kernel(q_ref, k_ref, v_ref, crow0_ref, crow1_ref, o_ref, m_scr, l_scr, acc_scr, *, tq):
    qi = pl.program_id(2)
    lane = lax.broadcasted_iota(jnp.int32, (tq, LANES), 1)
    q = q_ref[...]
    zero = jnp.zeros_like(q)
    qh = (jnp.where(lane < HEAD_D, q, zero), jnp.where(lane >= HEAD_D, q, zero))
    crow = (crow0_ref, crow1_ref)
    m_scr[...] = jnp.full(m_scr.shape, NEG_INF, F32)
    l_scr[...] = jnp.zeros(l_scr.shape, F32)
    acc_scr[...] = jnp.zeros(acc_scr.shape, F32)
    rows = lax.broadcasted_iota(jnp.int32, (tq, tq), 0)
    cols = lax.broadcasted_iota(jnp.int32, (tq, tq), 1)

    def step(j, masked):
        ks = pl.multiple_of(j * tq, tq)
        kb = k_ref[pl.ds(ks, tq), :]
        vb = v_ref[pl.ds(ks, tq), :]
        scores = [lax.dot_general(qh[a], kb, (((1,), (1,)), ((), ())), preferred_element_type=F32)
                  for a in range(2)]
        probs, alphas = [], []
        for a in range(2):
            s = scores[a] - crow[a][:, pl.ds(ks, tq)]
            if masked:
                s = jnp.where(cols <= rows, s, NEG_INF)
            m_prev = m_scr[a]
            m_next = jnp.maximum(m_prev, jnp.max(s, axis=1, keepdims=True))
            p = jnp.exp(s - jnp.concatenate([m_next] * (tq // LANES), axis=1))
            alpha = jnp.exp(m_prev - m_next)
            l_scr[a] = alpha * l_scr[a] + jnp.sum(p, axis=1, keepdims=True)
            m_scr[a] = m_next
            probs.append(p.astype(BF16))
            alphas.append(alpha)
        for a in range(2):
            acc_scr[a] = alphas[a] * acc_scr[a] + jnp.dot(probs[a], vb, preferred_element_type=F32)

    def body(j, c):
        step(j, False)
        return c

    lax.fori_loop(0, qi, body, 0)
    step(qi, True)
    o0 = acc_scr[0] / l_scr[0]
    o1 = acc_scr[1] / l_scr[1]
    o_ref[...] = jnp.where(lane < HEAD_D, o0, o1).astype(BF16)


def _fox(fq, fk, fv, crow, b, s, tq):
    nq = s // tq
    qspec = pl.BlockSpec((tq, LANES), lambda bi, hp, qi: (bi * nq + qi, hp))
    kvspec = pl.BlockSpec((s, LANES), lambda bi, hp, qi: (bi, hp))
    return pl.pallas_call(
        functools.partial(_fox_kernel, tq=tq),
        grid=(b, FOX_HEADS // 2, nq),
        in_specs=[qspec, kvspec, kvspec,
                  pl.BlockSpec((None, 1, s), lambda bi, hp, qi: (bi * FOX_HEADS + 2 * hp, 0, 0)),
                  pl.BlockSpec((None, 1, s), lambda bi, hp, qi: (bi * FOX_HEADS + 2 * hp + 1, 0, 0))],
        out_specs=qspec,
        out_shape=jax.ShapeDtypeStruct(fq.shape, BF16),
        scratch_shapes=[pltpu.VMEM((2, tq, LANES), F32), pltpu.VMEM((2, tq, LANES), F32),
                        pltpu.VMEM((2, tq, LANES), F32)],
        compiler_params=_cparams(("parallel", "parallel", "arbitrary")),
        name="fox",
    )(fq, fk, fv, crow, crow)


def _swa_kernel(sink_ref, q_ref, kp_ref, kc_ref, vp_ref, vc_ref, o_ref):
    blk = pl.program_id(1)
    lane = lax.broadcasted_iota(jnp.int32, (BLOCK, LANES), 1)
    kb = jnp.concatenate([kp_ref[...], kc_ref[...]], axis=0)
    vb = jnp.concatenate([vp_ref[...], vc_ref[...]], axis=0)
    qi = lax.broadcasted_iota(jnp.int32, (BLOCK, 2 * BLOCK), 0)
    kj = lax.broadcasted_iota(jnp.int32, (BLOCK, 2 * BLOCK), 1)
    dist = qi + BLOCK - kj
    ok = (dist >= 0) & (dist < BLOCK) & ((blk > 0) | (kj >= BLOCK))
    distf = dist.astype(F32)
    rep = SWA_HEADS // SWA_KV_HEADS
    scores = []
    for h in range(SWA_HEADS):
        q128 = q_ref[:, (h // 2) * LANES:(h // 2 + 1) * LANES]
        grp = h // rep
        qa = q128 if h % 2 == grp else pltpu.roll(q128, HEAD_D, axis=1)
        in_grp = (lane >= grp * HEAD_D) & (lane < (grp + 1) * HEAD_D)
        qa = jnp.where(in_grp, qa, jnp.zeros_like(qa))
        scores.append(lax.dot_general(qa, kb, (((1,), (1,)), ((), ())), preferred_element_type=F32))
    probs, dens = [], []
    for h in range(SWA_HEADS):
        s = jnp.where(ok, scores[h] - (2.0 ** -(h + 1)) * distf, NEG_INF)
        sink = sink_ref[h]
        m = jnp.maximum(jnp.max(s, axis=1, keepdims=True), sink)
        p = jnp.exp(s - m)
        dens.append(jnp.sum(p, axis=1, keepdims=True) + jnp.exp(sink - m))
        probs.append(p.astype(BF16))
    outs = []
    for h in range(SWA_HEADS):
        o = jnp.dot(probs[h], vb, preferred_element_type=F32) / dens[h]
        outs.append(o if h % 2 == h // rep else pltpu.roll(o, HEAD_D, axis=1))
    for pair in range(SWA_HEADS // 2):
        o_ref[:, pair * LANES:(pair + 1) * LANES] = jnp.where(lane < HEAD_D, outs[2 * pair],
                                                             outs[2 * pair + 1]).astype(BF16)


def _swa(sinks, sq, sk, sv, b, s):
    nb = s // BLOCK
    cur = lambda bi, i: (bi * nb + i, 0)
    prev = lambda bi, i: (bi * nb + jnp.maximum(i - 1, 0), 0)
    return pl.pallas_call(
        _swa_kernel,
        grid=(b, nb),
        in_specs=[pl.BlockSpec(memory_space=pltpu.SMEM),
                  pl.BlockSpec((BLOCK, 512), cur),
                  pl.BlockSpec((BLOCK, LANES), prev), pl.BlockSpec((BLOCK, LANES), cur),
                  pl.BlockSpec((BLOCK, LANES), prev), pl.BlockSpec((BLOCK, LANES), cur)],
        out_specs=pl.BlockSpec((BLOCK, 512), cur),
        out_shape=jax.ShapeDtypeStruct(sq.shape, BF16),
        compiler_params=_cparams(("parallel", "parallel")),
        name="swa",
    )(sinks, sq, sk, sk, sv, sv)


def _memkv_kernel(mem_ref, g_ref, w_ref, gk_ref, mk_ref, mv_ref):
    mn = _rms(mem_ref[...], g_ref[...]).astype(BF16)
    seg128 = _seg_matrix(MEM_HD)
    width = MEM_HEADS * MEM_HD
    for hd in range(MEM_HEADS):
        z = jnp.dot(mn, w_ref[:, hd * LANES:(hd + 1) * LANES], preferred_element_type=F32)
        mk_ref[:, hd * LANES:(hd + 1) * LANES] = _seg_norm(z, seg128, gk_ref[...]).astype(BF16)
    mv_ref[...] = jnp.dot(mn, w_ref[:, width:2 * width], preferred_element_type=F32).astype(BF16)


def _memkv(memf, g, w, gk):
    n = memf.shape[0]
    width = MEM_HEADS * MEM_HD
    tm = N_MEM
    return pl.pallas_call(
        _memkv_kernel,
        grid=(n // tm,),
        in_specs=[pl.BlockSpec((tm, D_MODEL), lambda i: (i, 0)), _whole(), _whole(), _whole()],
        out_specs=[pl.BlockSpec((tm, width), lambda i: (i, 0)), pl.BlockSpec((tm, width), lambda i: (i, 0))],
        out_shape=[jax.ShapeDtypeStruct((n, width), BF16), jax.ShapeDtypeStruct((n, width), BF16)],
        compiler_params=_cparams(("parallel",)),
        name="memkv",
    )(memf, g, w, gk)


def _mem_kernel(q_ref, k_ref, v_ref, o_ref):
    heads = [slice(hd * LANES, (hd + 1) * LANES) for hd in range(MEM_HEADS)]
    scores = [lax.dot_general(q_ref[:, sl], k_ref[:, sl], (((1,), (1,)), ((), ())), preferred_element_type=F32)
              for sl in heads]
    probs, dens = [], []
    for s in scores:
        p = jnp.exp(s - jnp.max(s, axis=1, keepdims=True))
        dens.append(jnp.sum(p, axis=1, keepdims=True))
        probs.append(p.astype(BF16))
    for sl, p, den in zip(heads, probs, dens):
        o_ref[:, sl] = (jnp.dot(p, v_ref[:, sl], preferred_element_type=F32) / den).astype(BF16)


def _mem(mq, mk, mv, b, s, tq):
    nq = s // tq
    width = MEM_HEADS * MEM_HD
    qspec = pl.BlockSpec((tq, width), lambda bi, qi: (bi * nq + qi, 0))
    kvspec = pl.BlockSpec((N_MEM, width), lambda bi, qi: (bi, 0))
    return pl.pallas_call(
        _mem_kernel,
        grid=(b, nq),
        in_specs=[qspec, kvspec, kvspec],
        out_specs=qspec,
        out_shape=jax.ShapeDtypeStruct(mq.shape, BF16),
        compiler_params=_cparams(("parallel", "parallel")),
        name="mem",
    )(mq, mk, mv)


def _merge_kernel(x_ref, of_ref, os_ref, om_ref, g1_ref, wg_ref, bg_ref, wfo_ref, wso_ref, wmo_ref, wout_ref,
                  g2_ref, wpq_ref, x1_ref, h2_ref, qp_ref):
    x = x_ref[...]
    h = _rms(x, g1_ref[...]).astype(BF16)
    merged = jnp.zeros(x.shape, F32)
    for br, (o_ref, w_ref) in enumerate(((of_ref, wfo_ref), (os_ref, wso_ref), (om_ref, wmo_ref))):
        sl = slice(br * D_MODEL, (br + 1) * D_MODEL)
        gate = _sigmoid(jnp.dot(h, wg_ref[:, sl], preferred_element_type=F32) + bg_ref[:, sl])
        merged = merged + gate * jnp.dot(o_ref[...], w_ref[...], preferred_element_type=F32)
    x1 = x + jnp.dot(merged.astype(BF16), wout_ref[...], preferred_element_type=F32)
    x1_ref[...] = x1
    h2 = _rms(x1, g2_ref[...])
    h2_ref[...] = h2
    qp_ref[...] = jnp.dot(h2.astype(BF16), wpq_ref[...], preferred_element_type=F32).astype(BF16)


def _merge(xf, of, os_, om, g1, wg, bg, wfo, wso, wmo, wout, g2, wpq, tm):
    n = xf.shape[0]
    row = lambda w: pl.BlockSpec((tm, w), lambda i: (i, 0))
    qw = wpq.shape[1]
    return pl.pallas_call(
        _merge_kernel,
        grid=(n // tm,),
        in_specs=[row(D_MODEL), row(512), row(512), row(512)] + [_whole()] * 9,
        out_specs=[row(D_MODEL), row(D_MODEL), row(qw)],
        out_shape=[jax.ShapeDtypeStruct((n, D_MODEL), F32), jax.ShapeDtypeStruct((n, D_MODEL), F32),
                   jax.ShapeDtypeStruct((n, qw), BF16)],
        compiler_params=_cparams(("parallel",)),
        name="merge",
    )(xf, of, os_, om, g1, wg, bg, wfo, wso, wmo, wout, g2, wpq)


def _topk_rows(s, k, order, payload=None):
    big = jnp.float32(2 ** 30)
    vals, sels = [], []
    for _ in range(k):
        m = jnp.max(s, axis=0, keepdims=True)
        sel = jnp.min(jnp.where(s == m, order, big), axis=0, keepdims=True)
        hit = order == sel
        vals.append(m)
        if payload is None:
            sels.append(sel)
        else:
            sels.append(jnp.max(jnp.where(hit, payload, -1.0), axis=0, keepdims=True))
        s = jnp.where(hit, -jnp.inf, s)
    return jnp.concatenate(vals, axis=0), jnp.concatenate(sels, axis=0)


def _topk_keys(s, k):
    depth = 4
    width = s.shape[0] // depth
    rid = lax.broadcasted_iota(jnp.int32, (width, s.shape[1]), 0).astype(F32)
    val = [s[i * width:(i + 1) * width] for i in range(depth)]
    idx = [rid + float(i * width) for i in range(depth)]

    def exchange(a, b, index_ordered):
        swap = val[b] > val[a]
        if not index_ordered:
            swap = swap | ((val[b] == val[a]) & (idx[b] < idx[a]))
        val[a], val[b] = jnp.where(swap, val[b], val[a]), jnp.where(swap, val[a], val[b])
        idx[a], idx[b] = jnp.where(swap, idx[b], idx[a]), jnp.where(swap, idx[a], idx[b])

    exchange(0, 1, True)
    exchange(2, 3, True)
    exchange(0, 2, True)
    exchange(1, 3, True)
    exchange(1, 2, False)

    big = jnp.float32(2 ** 30)
    vals, sels = [], []
    for _ in range(k):
        m = jnp.max(val[0], axis=0, keepdims=True)
        sel = jnp.min(jnp.where(val[0] == m, idx[0], big), axis=0, keepdims=True)
        hit = idx[0] == sel
        vals.append(m)
        sels.append(sel)
        for d in range(depth - 1):
            val[d] = jnp.where(hit, val[d + 1], val[d])
            idx[d] = jnp.where(hit, idx[d + 1], idx[d])
        val[depth - 1] = jnp.where(hit, -jnp.inf, val[depth - 1])
    return jnp.concatenate(vals, axis=0), jnp.concatenate(sels, axis=0)


def _pair_candidates(v1, i1, v2, i2):
    t = v1.shape[1]
    sub = lax.broadcasted_iota(jnp.int32, (8, t), 0)
    low = sub < 4
    b_lo = jnp.where(low, sub, sub - 4)
    v2a, i2a = v2[0:8], i2[0:8]
    v2r, i2r = pltpu.roll(v2a, 4, axis=0), pltpu.roll(i2a, 4, axis=0)
    vals, flat, eidx = [], [], []

    def single(a, v2x, i2x, b0, nvalid):
        vals.append(jnp.where(sub < nvalid, v1[a:a + 1] + v2x, -jnp.inf))
        flat.append((a * PEER_TOPK + b0 + sub).astype(F32))
        eidx.append(i1[a:a + 1] * N_KEYS + i2x)

    def double(a, nvalid_a, nvalid_b):
        va = jnp.where(low, v1[a:a + 1], v1[a + 1:a + 2]) + jnp.where(low, v2a, v2r)
        ok = b_lo < jnp.where(low, nvalid_a, nvalid_b)
        vals.append(jnp.where(ok, va, -jnp.inf))
        flat.append((jnp.where(low, a * PEER_TOPK, (a + 1) * PEER_TOPK) + b_lo).astype(F32))
        eidx.append(jnp.where(low, i1[a:a + 1], i1[a + 1:a + 2]) * N_KEYS + jnp.where(low, i2a, i2r))

    single(0, v2a, i2a, 0, 8)
    single(0, v2[8:16], i2[8:16], 8, 8)
    single(1, v2a, i2a, 0, 8)
    single(2, v2a, i2a, 0, 5)
    single(3, v2a, i2a, 0, 4)
    double(4, 3, 2)
    double(6, 2, 2)
    vals.append(v1[8:16] + v2[0:1])
    flat.append(((8 + sub) * PEER_TOPK).astype(F32))
    eidx.append(i1[8:16] * N_KEYS + i2[0:1])
    cat = lambda xs: jnp.concatenate(xs, axis=0)
    return cat(vals), cat(flat), cat(eidx)


def _topk_kernel(qp_ref, k1_ref, k2_ref, e_ref, g_ref):
    nt = (((1,), (1,)), ((), ()))
    for hd in range(PEER_HEADS):
        c0 = hd * 2 * N_KEYS
        s1 = lax.dot_general(k1_ref[hd], qp_ref[:, c0:c0 + N_KEYS], nt, preferred_element_type=F32)
        s2 = lax.dot_general(k2_ref[hd], qp_ref[:, c0 + N_KEYS:c0 + 2 * N_KEYS], nt, preferred_element_type=F32)
        v1, i1 = _topk_keys(s1, PEER_TOPK)
        v2, i2 = _topk_keys(s2, PEER_TOPK)
        cand, flat, eidx = _pair_candidates(v1, i1, v2, i2)
        sv, ex = _topk_rows(cand, PEER_TOPK, flat, payload=eidx)
        p = jnp.exp(sv - sv[0:1, :])
        g_ref[hd * PEER_TOPK:(hd + 1) * PEER_TOPK, :] = p / jnp.sum(p, axis=0, keepdims=True)
        e_ref[hd * PEER_TOPK:(hd + 1) * PEER_TOPK, :] = ex.astype(jnp.int32)


def _topk(qp, k1, k2, tb):
    n = qp.shape[0]
    nb = n // tb
    oblk = pl.BlockSpec((None, N_ACT, tb), lambda i: (i, 0, 0))
    return pl.pallas_call(
        _topk_kernel,
        grid=(nb,),
        in_specs=[pl.BlockSpec((tb, qp.shape[1]), lambda i: (i, 0)), _whole(), _whole()],
        out_specs=[oblk, oblk],
        out_shape=[jax.ShapeDtypeStruct((nb, N_ACT, tb), jnp.int32), jax.ShapeDtypeStruct((nb, N_ACT, tb), F32)],
        compiler_params=_cparams(("parallel",)),
        name="topk",
    )(qp, k1, k2)


def _pack_table(w):
    wb = w.astype(BF16)
    half = w.shape[1] // 2
    lo = lax.bitcast_convert_type(wb[:, :half], jnp.uint16).astype(jnp.uint32)
    hi = lax.bitcast_convert_type(wb[:, half:], jnp.uint16).astype(jnp.uint32)
    pk = lax.bitcast_convert_type(lo | (hi << 16), jnp.int32)
    return pk.reshape(w.shape[0] * ROWS_PER_EXPERT, LANES)


GATHER_BUFS = 16
SPLIT_ROWS = 8
NT_DIMS = (((1,), (1,)), ((), ()))


def _group_index_layout(e_tok, grp):
    nb, tb, n_act = e_tok.shape
    seg = n_act // grp
    x = e_tok.reshape(nb, tb // grp, grp, grp, seg)
    return jnp.swapaxes(x, 2, 3).reshape(nb, tb, n_act)


GATHER_CHUNKS = 4
PEER_U_SPAN = 0.85
PEER_V_SPAN = 0.6


def _gather_rows(idx_ref, tab_ref, g_ref, g0, k, grp, chunk=None):
    seg = N_ACT // grp
    rows = [idx_ref.at[g0 + kk] for kk in range(grp)]
    per = seg // GATHER_CHUNKS
    jos = range(seg) if chunk is None else range(chunk * per, (chunk + 1) * per)
    for jo in jos:
        for kk in range(grp):
            j = kk * seg + jo
            e4 = pl.multiple_of(rows[kk][k * seg + jo], ROWS_PER_EXPERT)
            g_ref[ROWS_PER_EXPERT * j:ROWS_PER_EXPERT * (j + 1), :] = tab_ref[pl.ds(e4, ROWS_PER_EXPERT), :]


def _token_pipeline(tb, idx_ref, tab_ref, compute, bufs, compute_span):
    grp = len(bufs) // 2
    sets = (bufs[:grp], bufs[grp:])
    for k in range(grp):
        _gather_rows(idx_ref, tab_ref, sets[0][k], 0, k, grp)

    def phase(src, dst, t0):
        nxt = jnp.minimum(t0 + grp, tb - grp)
        steps = [step for k in range(grp) for step in [compute(t0 + k, src[k])] * ROWS_PER_EXPERT]
        n_gather = grp * GATHER_CHUNKS
        done = 0
        for i in range(n_gather):
            while done < len(steps) and int(done * compute_span * n_gather / len(steps)) <= i:
                next(steps[done], None)
                done += 1
            _gather_rows(idx_ref, tab_ref, dst[i // GATHER_CHUNKS], nxt, i // GATHER_CHUNKS, grp, i % GATHER_CHUNKS)
        for gen in steps[done:]:
            next(gen, None)

    def body(it, c):
        t0 = it * grp
        for parity in range(2):
            @pl.when(it % 2 == parity)
            def _():
                phase(sets[parity], sets[1 - parity], t0)
        return c

    lax.fori_loop(0, tb // grp, body, 0)


def _expert_weights(g_ref, s):
    return pltpu.bitcast(g_ref[pl.ds(s, N_ACT, stride=ROWS_PER_EXPERT), :], BF16)


def _split_rows(ref, tb, rows):
    @pl.when(pl.program_id(0) == 0)
    def _():
        ref[...] = jnp.zeros(ref.shape, ref.dtype)

    for r, val in enumerate(rows):
        for c in range(ref.shape[0]):
            ref.at[c][pl.ds(r, tb, stride=SPLIT_ROWS), :] = val[:, c * LANES:(c + 1) * LANES]


def _merge_rows(ref, tb, r):
    return jnp.concatenate([ref.at[c][pl.ds(r, tb, stride=SPLIT_ROWS), :] for c in range(ref.shape[0])], axis=1)


def _load_tile(ref, t8):
    return jnp.concatenate([ref.at[c][pl.ds(t8, SPLIT_ROWS), :] for c in range(ref.shape[0])], axis=1)


def _store_tile(ref, t8, val):
    for c in range(ref.shape[0]):
        ref.at[c][pl.ds(t8, SPLIT_ROWS), :] = val[:, c * LANES:(c + 1) * LANES]


def _tile_scratch(tb, width):
    return pltpu.VMEM((width // LANES, tb * SPLIT_ROWS, LANES), F32)


def _hi_lo(x):
    hi = x.astype(BF16).astype(F32)
    return hi, x - hi


def _peer_u_body(idx_ref, h_ref, gate_ref, tab_ref, act_ref, *scratch, tb):
    bufs, (r_ref,) = scratch[:GATHER_BUFS], scratch[GATHER_BUFS:]
    half = D_MODEL // 2
    ones = jnp.ones((SPLIT_ROWS, 2 * LANES), BF16)

    def compute(t, g_ref):
        t8 = pl.multiple_of(t * SPLIT_ROWS, SPLIT_ROWS)
        acc = jnp.zeros((N_ACT, LANES), F32)
        hrow = h_ref[pl.ds(t, 1), :]
        for s in range(ROWS_PER_EXPERT):
            w = g_ref[pl.ds(s, N_ACT, stride=ROWS_PER_EXPERT), :]
            lo = pltpu.bitcast(w << 16, F32)
            hi = pltpu.bitcast(w & jnp.int32(-65536), F32)
            acc = (acc + lo * hrow[:, s * LANES:(s + 1) * LANES]
                   + hi * hrow[:, half + s * LANES:half + (s + 1) * LANES])
            if s == ROWS_PER_EXPERT - 1:
                a_hi, a_lo = _hi_lo(acc)
                terms = jnp.concatenate([a_hi.astype(BF16), a_lo.astype(BF16)], axis=1)
                r_ref[0, pl.ds(t8, SPLIT_ROWS), :] = lax.dot_general(ones, terms, NT_DIMS,
                                                                     preferred_element_type=F32)
            yield

    _token_pipeline(tb, idx_ref, tab_ref, compute, bufs, compute_span=PEER_U_SPAN)
    a = _merge_rows(r_ref, tb, 0)
    act_ref[...] = _gelu(a) * jnp.transpose(gate_ref[...])


def _peer_u(idx, h2, gates_t, upk, tb):
    nb = idx.shape[0]
    return pl.pallas_call(
        functools.partial(_peer_u_body, tb=tb),
        grid=(nb,),
        in_specs=[pl.BlockSpec((None, tb, N_ACT), lambda i: (i, 0, 0), memory_space=pltpu.SMEM),
                  pl.BlockSpec((tb, D_MODEL), lambda i: (i, 0)),
                  pl.BlockSpec((None, N_ACT, tb), lambda i: (i, 0, 0)), _whole()],
        out_specs=pl.BlockSpec((None, tb, N_ACT), lambda i: (i, 0, 0)),
        out_shape=jax.ShapeDtypeStruct((nb, tb, N_ACT), F32),
        scratch_shapes=[pltpu.VMEM((N_ACT * ROWS_PER_EXPERT, LANES), jnp.int32)] * GATHER_BUFS
        + [_tile_scratch(tb, N_ACT)],
        compiler_params=_cparams(("arbitrary",)),
        name="peer_u",
    )(idx, h2, gates_t, upk)


def _peer_v_body(idx_ref, act_ref, x1_ref, tab_ref, out_ref, *scratch, tb):
    bufs, (l_ref, o_ref) = scratch[:GATHER_BUFS], scratch[GATHER_BUFS:]
    ahi, alo = _hi_lo(act_ref[...])
    ahi, alo = ahi.astype(BF16), alo.astype(BF16)
    j2 = 2 * lax.broadcasted_iota(jnp.int32, (N_ACT, 2 * N_ACT), 0)
    col = lax.broadcasted_iota(jnp.int32, (N_ACT, 2 * N_ACT), 1)
    spread = [jnp.where(col == j2 + b, 1.0, 0.0).astype(BF16) for b in range(2)]
    _split_rows(l_ref, tb, [jnp.dot(a, e, preferred_element_type=F32) for e in spread for a in (ahi, alo)])

    def compute(t, g_ref):
        t8 = pl.multiple_of(t * SPLIT_ROWS, SPLIT_ROWS)
        lhs = _load_tile(l_ref, t8).astype(BF16)
        for s in range(ROWS_PER_EXPERT):
            o_ref.at[s][pl.ds(t8, SPLIT_ROWS), :] = jnp.dot(lhs, _expert_weights(g_ref, s),
                                                             preferred_element_type=F32)
            yield

    _token_pipeline(tb, idx_ref, tab_ref, compute, bufs, compute_span=PEER_V_SPAN)
    o = [_merge_rows(o_ref, tb, k) for k in range(4)]
    out_ref[...] = x1_ref[...] + jnp.concatenate([o[0] + o[1], o[2] + o[3]], axis=1)


def _peer_v(idx, act, x1, vpk, tb):
    nb = idx.shape[0]
    return pl.pallas_call(
        functools.partial(_peer_v_body, tb=tb),
        grid=(nb,),
        in_specs=[pl.BlockSpec((None, tb, N_ACT), lambda i: (i, 0, 0), memory_space=pltpu.SMEM),
                  pl.BlockSpec((None, tb, N_ACT), lambda i: (i, 0, 0)),
                  pl.BlockSpec((tb, D_MODEL), lambda i: (i, 0)), _whole()],
        out_specs=pl.BlockSpec((tb, D_MODEL), lambda i: (i, 0)),
        out_shape=jax.ShapeDtypeStruct(x1.shape, F32),
        scratch_shapes=[pltpu.VMEM((N_ACT * ROWS_PER_EXPERT, LANES), jnp.int32)] * GATHER_BUFS
        + [_tile_scratch(tb, 2 * N_ACT), _tile_scratch(tb, D_MODEL // 2)],
        compiler_params=_cparams(("arbitrary",)),
        name="peer_v",
    )(idx, act, x1, vpk)


def _tile(n, pref):
    t = min(pref, n)
    while n % t:
        t //= 2
    return t


def kernel(x, mem, norm1_g, w_in, b_gate, b_forget, fox_q_g, fox_k_g, swa_q_g, swa_k_g, swa_sinks, mem_norm_g, w_mem_kv, mem_q_g, mem_k_g, w_fox_o, w_swa_o, w_mem_o, w_out, norm2_g, w_peer_q, peer_keys1, peer_keys2, peer_u, peer_v):
    b, s, d = x.shape
    n = b * s
    assert d == D_MODEL and s % 256 == 0 and mem.shape[1] == N_MEM
    depth = norm1_g.shape[0]
    xf = x.reshape(n, d)
    memf = mem.reshape(b * N_MEM, d)
    tb = BLOCK
    for l in range(depth):
        w = w_in[l]
        o_ff, o_sq, o_sk, o_sv, o_mq, o_gl = 1536, 1544, 2056, 2184, 2312, 2824
        w_att = jnp.concatenate(
            [w[:, 0:o_ff], w[:, o_sq:o_gl], w[:, o_ff:o_sq], jnp.zeros((d, LANES - FOX_HEADS), F32)],
            axis=1).astype(BF16)
        w_gate = w[:, o_gl:].astype(BF16)
        tile2 = lambda g: jnp.concatenate([g, g])
        gains = jnp.stack([tile2(fox_q_g[l]) * HEAD_D ** -0.5, tile2(fox_k_g[l]),
                           tile2(swa_q_g[l]) * HEAD_D ** -0.5, tile2(swa_k_g[l]),
                           mem_q_g[l] * MEM_HD ** -0.5] + [jnp.zeros((LANES,), F32)] * 3)
        bf = jnp.concatenate([b_forget[l], jnp.zeros((LANES - FOX_HEADS,), F32)]).reshape(1, LANES)
        g1 = norm1_g[l].reshape(1, d)

        fq, fk, fv, sq, sk, sv, mq, lf = _inproj(xf, g1, w_att, gains, bf, _tile(n, 512))
        crow = _cumsum(lf.reshape(b, s, LANES))
        o_fox = _fox(fq, fk, fv, crow.reshape(b * FOX_HEADS, 1, s), b, s, 512)
        o_swa = _swa(swa_sinks[l], sq, sk, sv, b, s)
        mk, mv = _memkv(memf, mem_norm_g[l].reshape(1, d), w_mem_kv[l].astype(BF16), mem_k_g[l].reshape(1, LANES))
        o_mem = _mem(mq, mk, mv, b, s, 256)

        x1, h2, qp = _merge(xf, o_fox, o_swa, o_mem, g1, w_gate, b_gate[l].reshape(1, 3 * d),
                            w_fox_o[l].astype(BF16), w_swa_o[l].astype(BF16), w_mem_o[l].astype(BF16),
                            w_out[l].astype(BF16), norm2_g[l].reshape(1, d), w_peer_q[l].astype(BF16), _tile(n, 256))

        e_t, gates_t = _topk(qp, peer_keys1[l].astype(BF16), peer_keys2[l].astype(BF16), tb)
        idx = _group_index_layout(jnp.swapaxes(e_t, 1, 2) * ROWS_PER_EXPERT, GATHER_BUFS // 2)
        act_t = _peer_u(idx, h2, gates_t, _pack_table(peer_u[l]), tb)
        xf = _peer_v(idx, act_t, x1, _pack_table(peer_v[l]), tb)
    return xf.reshape(b, s, d)
```

```python
import functools

import jax
import jax.numpy as jnp
from jax import lax
from jax.experimental import pallas as pl
from jax.experimental.pallas import tpu as pltpu

F32 = jnp.float32
BF16 = jnp.bfloat16

D_MODEL = 1024
BLOCK = 128
FOX_HEADS = 8
HEAD_D = 64
SWA_HEADS = 8
SWA_KV_HEADS = 2
MEM_HEADS = 4
MEM_HD = 128
N_MEM = 256
PEER_HEADS = 8
N_KEYS = 128
N_EXPERTS = N_KEYS * N_KEYS
PEER_TOPK = 16
N_ACT = PEER_HEADS * PEER_TOPK
EPS = 1e-6
NEG_INF = -1e30

LANES = 128
ROWS_PER_EXPERT = 4
VMEM_LIMIT = 48 * 1024 * 1024

C_FQ, C_FK, C_FV, C_SQ, C_SK, C_SV, C_MQ, C_FF = 0, 512, 1024, 1536, 2048, 2176, 2304, 2816
ATT_COLS = 2944


def _cparams(sem):
    return pltpu.CompilerParams(dimension_semantics=sem, vmem_limit_bytes=VMEM_LIMIT)


def _whole():
    return pl.BlockSpec(memory_space=pltpu.VMEM)


def _rms(xf, g):
    return xf * lax.rsqrt(jnp.mean(xf * xf, axis=-1, keepdims=True) + EPS) * g


def _seg_matrix(seg):
    r = (lax.broadcasted_iota(jnp.int32, (2 * LANES, LANES), 0) % LANES) // seg
    c = lax.broadcasted_iota(jnp.int32, (2 * LANES, LANES), 1) // seg
    return jnp.where(r == c, 1.0 / seg, 0.0).astype(BF16)


def _seg_norm(z, seg_mat, gain):
    sq = z * z
    hi = sq.astype(BF16)
    lo = (sq - hi.astype(F32)).astype(BF16)
    ms = jnp.dot(jnp.concatenate([hi, lo], axis=1), seg_mat, preferred_element_type=F32)
    return z * lax.rsqrt(ms + EPS) * gain


def _log_sigmoid(x):
    return jnp.minimum(x, 0.0) - jnp.log1p(jnp.exp(-jnp.abs(x)))


def _sigmoid(x):
    return 1.0 / (1.0 + jnp.exp(-x))


def _gelu(x):
    return 0.5 * x * (1.0 + lax.erf(x * 0.7071067811865476))


def _inproj_kernel(x_ref, g1_ref, w_ref, gains_ref, bf_ref,
                   fq_ref, fk_ref, fv_ref, sq_ref, sk_ref, sv_ref, mq_ref, lf_ref):
    h = _rms(x_ref[...], g1_ref[...]).astype(BF16)
    seg64 = _seg_matrix(HEAD_D)
    seg128 = _seg_matrix(MEM_HD)

    blocks = []
    for out_ref, nblk, seg_mat, gain_row in ((fq_ref, 4, seg64, 0), (fk_ref, 4, seg64, 1), (fv_ref, 4, None, None),
                                            (sq_ref, 4, seg64, 2), (sk_ref, 1, seg64, 3), (sv_ref, 1, None, None),
                                            (mq_ref, 4, seg128, 4)):
        blocks += [(out_ref, b, seg_mat, gain_row) for b in range(nblk)]
    blocks.append((lf_ref, 0, None, None))
    assert len(blocks) * LANES == ATT_COLS

    def proj(c0):
        width = min(2 * LANES, ATT_COLS - c0)
        return jnp.dot(h, w_ref[:, c0:c0 + width], preferred_element_type=F32)

    def finish(z, blk):
        out_ref, b, seg_mat, gain_row = blk
        if out_ref is lf_ref:
            out_ref[...] = _log_sigmoid(z + bf_ref[...])
        elif seg_mat is None:
            out_ref[:, b * LANES:(b + 1) * LANES] = z.astype(BF16)
        else:
            gain = gains_ref[gain_row:gain_row + 1, :]
            out_ref[:, b * LANES:(b + 1) * LANES] = _seg_norm(z, seg_mat, gain).astype(BF16)

    starts = list(range(0, ATT_COLS, 2 * LANES))
    z = proj(starts[0])
    for i, c0 in enumerate(starts):
        z_next = proj(starts[i + 1]) if i + 1 < len(starts) else None
        for half in range(z.shape[1] // LANES):
            finish(z[:, half * LANES:(half + 1) * LANES], blocks[c0 // LANES + half])
        z = z_next


def _inproj(xf, g1, w_att, gains, bf, tm):
    n = xf.shape[0]
    row = lambda w: pl.BlockSpec((tm, w), lambda i: (i, 0))
    out_shape = [jax.ShapeDtypeStruct((n, w), BF16) for w in (512, 512, 512, 512, 128, 128, 512)]
    out_shape.append(jax.ShapeDtypeStruct((n, 128), F32))
    return pl.pallas_call(
        _inproj_kernel,
        grid=(n // tm,),
        in_specs=[row(D_MODEL), _whole(), _whole(), _whole(), _whole()],
        out_specs=[row(512), row(512), row(512), row(512), row(128), row(128), row(512), row(128)],
        out_shape=out_shape,
        compiler_params=_cparams(("parallel",)),
        name="inproj",
    )(xf, g1, w_att, gains, bf)


def _cumsum_kernel(lf_ref, crow_ref, ccol_ref, *, blk):
    s = lf_ref.shape[0]
    r = lax.broadcasted_iota(jnp.int32, (blk, blk), 0)
    c = lax.broadcasted_iota(jnp.int32, (blk, blk), 1)
    tri = jnp.where(c <= r, 1.0, 0.0).astype(F32)
    carry = jnp.zeros((1, LANES), F32)
    for b in range(s // blk):
        x = lf_ref[b * blk:(b + 1) * blk, :]
        cs = jnp.dot(tri, x, preferred_element_type=F32, precision=lax.Precision.HIGHEST) + carry
        ccol_ref[b * blk:(b + 1) * blk, :] = cs
        carry = cs[blk - 1:blk, :]
    ct = jnp.transpose(ccol_ref[...])
    crow_ref[...] = ct[0:FOX_HEADS, :]


def _cumsum(lf3):
    b, s, _ = lf3.shape
    return pl.pallas_call(
        functools.partial(_cumsum_kernel, blk=256),
        grid=(b,),
        in_specs=[pl.BlockSpec((None, s, LANES), lambda i: (i, 0, 0))],
        out_specs=pl.BlockSpec((None, FOX_HEADS, s), lambda i: (i, 0, 0)),
        out_shape=jax.ShapeDtypeStruct((b, FOX_HEADS, s), F32),
        scratch_shapes=[pltpu.VMEM((s, LANES), F32)],
        compiler_params=_cparams(("parallel",)),
        name="cumsum",
    )(lf3)


def _fox_kernel(q_ref, k_ref, v_ref, crow0_ref, crow1_ref, o_ref, m_scr, l_scr, acc_scr, *, tq):
    qi = pl.program_id(2)
    lane = lax.broadcasted_iota(jnp.int32, (tq, LANES), 1)
    q = q_ref[...]
    zero = jnp.zeros_like(q)
    qh = (jnp.where(lane < HEAD_D, q, zero), jnp.where(lane >= HEAD_D, q, zero))
    crow = (crow0_ref, crow1_ref)
    m_scr[...] = jnp.full(m_scr.shape, NEG_INF, F32)
    l_scr[...] = jnp.zeros(l_scr.shape, F32)
    acc_scr[...] = jnp.zeros(acc_scr.shape, F32)
    rows = lax.broadcasted_iota(jnp.int32, (tq, tq), 0)
    cols = lax.broadcasted_iota(jnp.int32, (tq, tq), 1)

    def step(j, masked):
        ks = pl.multiple_of(j * tq, tq)
        kb = k_ref[pl.ds(ks, tq), :]
        vb = v_ref[pl.ds(ks, tq), :]
        scores = [lax.dot_general(qh[a], kb, (((1,), (1,)), ((), ())), preferred_element_type=F32)
                  for a in range(2)]
        probs, alphas = [], []
        for a in range(2):
            s = scores[a] - crow[a][:, pl.ds(ks, tq)]
            if masked:
                s = jnp.where(cols <= rows, s, NEG_INF)
            m_prev = m_scr[a]
            m_next = jnp.maximum(m_prev, jnp.max(s, axis=1, keepdims=True))
            p = jnp.exp(s - jnp.concatenate([m_next] * (tq // LANES), axis=1))
            alpha = jnp.exp(m_prev - m_next)
            l_scr[a] = alpha * l_scr[a] + jnp.sum(p, axis=1, keepdims=True)
            m_scr[a] = m_next
            probs.append(p.astype(BF16))
            alphas.append(alpha)
        for a in range(2):
            acc_scr[a] = alphas[a] * acc_scr[a] + jnp.dot(probs[a], vb, preferred_element_type=F32)

    def body(j, c):
        step(j, False)
        return c

    lax.fori_loop(0, qi, body, 0)
    step(qi, True)
    o0 = acc_scr[0] / l_scr[0]
    o1 = acc_scr[1] / l_scr[1]
    o_ref[...] = jnp.where(lane < HEAD_D, o0, o1).astype(BF16)


def _fox(fq, fk, fv, crow, b, s, tq):
    nq = s // tq
    qspec = pl.BlockSpec((tq, LANES), lambda bi, hp, qi: (bi * nq + qi, hp))
    kvspec = pl.BlockSpec((s, LANES), lambda bi, hp, qi: (bi, hp))
    return pl.pallas_call(
        functools.partial(_fox_kernel, tq=tq),
        grid=(b, FOX_HEADS // 2, nq),
        in_specs=[qspec, kvspec, kvspec,
                  pl.BlockSpec((None, 1, s), lambda bi, hp, qi: (bi * FOX_HEADS + 2 * hp, 0, 0)),
                  pl.BlockSpec((None, 1, s), lambda bi, hp, qi: (bi * FOX_HEADS + 2 * hp + 1, 0, 0))],
        out_specs=qspec,
        out_shape=jax.ShapeDtypeStruct(fq.shape, BF16),
        scratch_shapes=[pltpu.VMEM((2, tq, LANES), F32), pltpu.VMEM((2, tq, LANES), F32),
                        pltpu.VMEM((2, tq, LANES), F32)],
        compiler_params=_cparams(("parallel", "parallel", "arbitrary")),
        name="fox",
    )(fq, fk, fv, crow, crow)


def _swa_kernel(sink_ref, q_ref, kp_ref, kc_ref, vp_ref, vc_ref, o_ref):
    blk = pl.program_id(1)
    lane = lax.broadcasted_iota(jnp.int32, (BLOCK, LANES), 1)
    kb = jnp.concatenate([kp_ref[...], kc_ref[...]], axis=0)
    vb = jnp.concatenate([vp_ref[...], vc_ref[...]], axis=0)
    qi = lax.broadcasted_iota(jnp.int32, (BLOCK, 2 * BLOCK), 0)
    kj = lax.broadcasted_iota(jnp.int32, (BLOCK, 2 * BLOCK), 1)
    dist = qi + BLOCK - kj
    ok = (dist >= 0) & (dist < BLOCK) & ((blk > 0) | (kj >= BLOCK))
    distf = dist.astype(F32)
    rep = SWA_HEADS // SWA_KV_HEADS
    scores = []
    for h in range(SWA_HEADS):
        q128 = q_ref[:, (h // 2) * LANES:(h // 2 + 1) * LANES]
        grp = h // rep
        qa = q128 if h % 2 == grp else pltpu.roll(q128, HEAD_D, axis=1)
        in_grp = (lane >= grp * HEAD_D) & (lane < (grp + 1) * HEAD_D)
        qa = jnp.where(in_grp, qa, jnp.zeros_like(qa))
        scores.append(lax.dot_general(qa, kb, (((1,), (1,)), ((), ())), preferred_element_type=F32))
    probs, dens = [], []
    for h in range(SWA_HEADS):
        s = jnp.where(ok, scores[h] - (2.0 ** -(h + 1)) * distf, NEG_INF)
        sink = sink_ref[h]
        m = jnp.maximum(jnp.max(s, axis=1, keepdims=True), sink)
        p = jnp.exp(s - m)
        dens.append(jnp.sum(p, axis=1, keepdims=True) + jnp.exp(sink - m))
        probs.append(p.astype(BF16))
    outs = []
    for h in range(SWA_HEADS):
        o = jnp.dot(probs[h], vb, preferred_element_type=F32) / dens[h]
        outs.append(o if h % 2 == h // rep else pltpu.roll(o, HEAD_D, axis=1))
    for pair in range(SWA_HEADS // 2):
        o_ref[:, pair * LANES:(pair + 1) * LANES] = jnp.where(lane < HEAD_D, outs[2 * pair],
                                                             outs[2 * pair + 1]).astype(BF16)


def _swa(sinks, sq, sk, sv, b, s):
    nb = s // BLOCK
    cur = lambda bi, i: (bi * nb + i, 0)
    prev = lambda bi, i: (bi * nb + jnp.maximum(i - 1, 0), 0)
    return pl.pallas_call(
        _swa_kernel,
        grid=(b, nb),
        in_specs=[pl.BlockSpec(memory_space=pltpu.SMEM),
                  pl.BlockSpec((BLOCK, 512), cur),
                  pl.BlockSpec((BLOCK, LANES), prev), pl.BlockSpec((BLOCK, LANES), cur),
                  pl.BlockSpec((BLOCK, LANES), prev), pl.BlockSpec((BLOCK, LANES), cur)],
        out_specs=pl.BlockSpec((BLOCK, 512), cur),
        out_shape=jax.ShapeDtypeStruct(sq.shape, BF16),
        compiler_params=_cparams(("parallel", "parallel")),
        name="swa",
    )(sinks, sq, sk, sk, sv, sv)


def _memkv_kernel(mem_ref, g_ref, w_ref, gk_ref, mk_ref, mv_ref):
    mn = _rms(mem_ref[...], g_ref[...]).astype(BF16)
    seg128 = _seg_matrix(MEM_HD)
    width = MEM_HEADS * MEM_HD
    for hd in range(MEM_HEADS):
        z = jnp.dot(mn, w_ref[:, hd * LANES:(hd + 1) * LANES], preferred_element_type=F32)
        mk_ref[:, hd * LANES:(hd + 1) * LANES] = _seg_norm(z, seg128, gk_ref[...]).astype(BF16)
    mv_ref[...] = jnp.dot(mn, w_ref[:, width:2 * width], preferred_element_type=F32).astype(BF16)


def _memkv(memf, g, w, gk):
    n = memf.shape[0]
    width = MEM_HEADS * MEM_HD
    tm = N_MEM
    return pl.pallas_call(
        _memkv_kernel,
        grid=(n // tm,),
        in_specs=[pl.BlockSpec((tm, D_MODEL), lambda i: (i, 0)), _whole(), _whole(), _whole()],
        out_specs=[pl.BlockSpec((tm, width), lambda i: (i, 0)), pl.BlockSpec((tm, width), lambda i: (i, 0))],
        out_shape=[jax.ShapeDtypeStruct((n, width), BF16), jax.ShapeDtypeStruct((n, width), BF16)],
        compiler_params=_cparams(("parallel",)),
        name="memkv",
    )(memf, g, w, gk)


def _mem_kernel(q_ref, k_ref, v_ref, o_ref):
    heads = [slice(hd * LANES, (hd + 1) * LANES) for hd in range(MEM_HEADS)]
    scores = [lax.dot_general(q_ref[:, sl], k_ref[:, sl], (((1,), (1,)), ((), ())), preferred_element_type=F32)
              for sl in heads]
    probs, dens = [], []
    for s in scores:
        p = jnp.exp(s - jnp.max(s, axis=1, keepdims=True))
        dens.append(jnp.sum(p, axis=1, keepdims=True))
        probs.append(p.astype(BF16))
    for sl, p, den in zip(heads, probs, dens):
        o_ref[:, sl] = (jnp.dot(p, v_ref[:, sl], preferred_element_type=F32) / den).astype(BF16)


def _mem(mq, mk, mv, b, s, tq):
    nq = s // tq
    width = MEM_HEADS * MEM_HD
    qspec = pl.BlockSpec((tq, width), lambda bi, qi: (bi * nq + qi, 0))
    kvspec = pl.BlockSpec((N_MEM, width), lambda bi, qi: (bi, 0))
    return pl.pallas_call(
        _mem_kernel,
        grid=(b, nq),
        in_specs=[qspec, kvspec, kvspec],
        out_specs=qspec,
        out_shape=jax.ShapeDtypeStruct(mq.shape, BF16),
        compiler_params=_cparams(("parallel", "parallel")),
        name="mem",
    )(mq, mk, mv)


def _merge_kernel(x_ref, of_ref, os_ref, om_ref, g1_ref, wg_ref, bg_ref, wfo_ref, wso_ref, wmo_ref, wout_ref,
                  g2_ref, wpq_ref, x1_ref, h2_ref, qp_ref):
    x = x_ref[...]
    h = _rms(x, g1_ref[...]).astype(BF16)
    merged = jnp.zeros(x.shape, F32)
    for br, (o_ref, w_ref) in enumerate(((of_ref, wfo_ref), (os_ref, wso_ref), (om_ref, wmo_ref))):
        sl = slice(br * D_MODEL, (br + 1) * D_MODEL)
        gate = _sigmoid(jnp.dot(h, wg_ref[:, sl], preferred_element_type=F32) + bg_ref[:, sl])
        merged = merged + gate * jnp.dot(o_ref[...], w_ref[...], preferred_element_type=F32)
    x1 = x + jnp.dot(merged.astype(BF16), wout_ref[...], preferred_element_type=F32)
    x1_ref[...] = x1
    h2 = _rms(x1, g2_ref[...])
    h2_ref[...] = h2
    qp_ref[...] = jnp.dot(h2.astype(BF16), wpq_ref[...], preferred_element_type=F32).astype(BF16)


def _merge(xf, of, os_, om, g1, wg, bg, wfo, wso, wmo, wout, g2, wpq, tm):
    n = xf.shape[0]
    row = lambda w: pl.BlockSpec((tm, w), lambda i: (i, 0))
    qw = wpq.shape[1]
    return pl.pallas_call(
        _merge_kernel,
        grid=(n // tm,),
        in_specs=[row(D_MODEL), row(512), row(512), row(512)] + [_whole()] * 9,
        out_specs=[row(D_MODEL), row(D_MODEL), row(qw)],
        out_shape=[jax.ShapeDtypeStruct((n, D_MODEL), F32), jax.ShapeDtypeStruct((n, D_MODEL), F32),
                   jax.ShapeDtypeStruct((n, qw), BF16)],
        compiler_params=_cparams(("parallel",)),
        name="merge",
    )(xf, of, os_, om, g1, wg, bg, wfo, wso, wmo, wout, g2, wpq)


def _topk_rows(s, k, order, payload=None):
    big = jnp.float32(2 ** 30)
    vals, sels = [], []
    for _ in range(k):
        m = jnp.max(s, axis=0, keepdims=True)
        sel = jnp.min(jnp.where(s == m, order, big), axis=0, keepdims=True)
        hit = order == sel
        vals.append(m)
        if payload is None:
            sels.append(sel)
        else:
            sels.append(jnp.max(jnp.where(hit, payload, -1.0), axis=0, keepdims=True))
        s = jnp.where(hit, -jnp.inf, s)
    return jnp.concatenate(vals, axis=0), jnp.concatenate(sels, axis=0)


def _topk_keys(s, k):
    depth = 4
    width = s.shape[0] // depth
    rid = lax.broadcasted_iota(jnp.int32, (width, s.shape[1]), 0).astype(F32)
    val = [s[i * width:(i + 1) * width] for i in range(depth)]
    idx = [rid + float(i * width) for i in range(depth)]

    def exchange(a, b, index_ordered):
        swap = val[b] > val[a]
        if not index_ordered:
            swap = swap | ((val[b] == val[a]) & (idx[b] < idx[a]))
        val[a], val[b] = jnp.where(swap, val[b], val[a]), jnp.where(swap, val[a], val[b])
        idx[a], idx[b] = jnp.where(swap, idx[b], idx[a]), jnp.where(swap, idx[a], idx[b])

    exchange(0, 1, True)
    exchange(2, 3, True)
    exchange(0, 2, True)
    exchange(1, 3, True)
    exchange(1, 2, False)

    big = jnp.float32(2 ** 30)
    vals, sels = [], []
    for _ in range(k):
        m = jnp.max(val[0], axis=0, keepdims=True)
        sel = jnp.min(jnp.where(val[0] == m, idx[0], big), axis=0, keepdims=True)
        hit = idx[0] == sel
        vals.append(m)
        sels.append(sel)
        for d in range(depth - 1):
            val[d] = jnp.where(hit, val[d + 1], val[d])
            idx[d] = jnp.where(hit, idx[d + 1], idx[d])
        val[depth - 1] = jnp.where(hit, -jnp.inf, val[depth - 1])
    return jnp.concatenate(vals, axis=0), jnp.concatenate(sels, axis=0)


def _pair_candidates(v1, i1, v2, i2):
    t = v1.shape[1]
    sub = lax.broadcasted_iota(jnp.int32, (8, t), 0)
    low = sub < 4
    b_lo = jnp.where(low, sub, sub - 4)
    v2a, i2a = v2[0:8], i2[0:8]
    v2r, i2r = pltpu.roll(v2a, 4, axis=0), pltpu.roll(i2a, 4, axis=0)
    vals, flat, eidx = [], [], []

    def single(a, v2x, i2x, b0, nvalid):
        vals.append(jnp.where(sub < nvalid, v1[a:a + 1] + v2x, -jnp.inf))
        flat.append((a * PEER_TOPK + b0 + sub).astype(F32))
        eidx.append(i1[a:a + 1] * N_KEYS + i2x)

    def double(a, nvalid_a, nvalid_b):
        va = jnp.where(low, v1[a:a + 1], v1[a + 1:a + 2]) + jnp.where(low, v2a, v2r)
        ok = b_lo < jnp.where(low, nvalid_a, nvalid_b)
        vals.append(jnp.where(ok, va, -jnp.inf))
        flat.append((jnp.where(low, a * PEER_TOPK, (a + 1) * PEER_TOPK) + b_lo).astype(F32))
        eidx.append(jnp.where(low, i1[a:a + 1], i1[a + 1:a + 2]) * N_KEYS + jnp.where(low, i2a, i2r))

    single(0, v2a, i2a, 0, 8)
    single(0, v2[8:16], i2[8:16], 8, 8)
    single(1, v2a, i2a, 0, 8)
    single(2, v2a, i2a, 0, 5)
    single(3, v2a, i2a, 0, 4)
    double(4, 3, 2)
    double(6, 2, 2)
    vals.append(v1[8:16] + v2[0:1])
    flat.append(((8 + sub) * PEER_TOPK).astype(F32))
    eidx.append(i1[8:16] * N_KEYS + i2[0:1])
    cat = lambda xs: jnp.concatenate(xs, axis=0)
    return cat(vals), cat(flat), cat(eidx)


def _topk_kernel(qp_ref, k1_ref, k2_ref, e_ref, g_ref):
    nt = (((1,), (1,)), ((), ()))
    for hd in range(PEER_HEADS):
        c0 = hd * 2 * N_KEYS
        s1 = lax.dot_general(k1_ref[hd], qp_ref[:, c0:c0 + N_KEYS], nt, preferred_element_type=F32)
        s2 = lax.dot_general(k2_ref[hd], qp_ref[:, c0 + N_KEYS:c0 + 2 * N_KEYS], nt, preferred_element_type=F32)
        v1, i1 = _topk_keys(s1, PEER_TOPK)
        v2, i2 = _topk_keys(s2, PEER_TOPK)
        cand, flat, eidx = _pair_candidates(v1, i1, v2, i2)
        sv, ex = _topk_rows(cand, PEER_TOPK, flat, payload=eidx)
        p = jnp.exp(sv - sv[0:1, :])
        g_ref[hd * PEER_TOPK:(hd + 1) * PEER_TOPK, :] = p / jnp.sum(p, axis=0, keepdims=True)
        e_ref[hd * PEER_TOPK:(hd + 1) * PEER_TOPK, :] = ex.astype(jnp.int32)


def _topk(qp, k1, k2, tb):
    n = qp.shape[0]
    nb = n // tb
    oblk = pl.BlockSpec((None, N_ACT, tb), lambda i: (i, 0, 0))
    return pl.pallas_call(
        _topk_kernel,
        grid=(nb,),
        in_specs=[pl.BlockSpec((tb, qp.shape[1]), lambda i: (i, 0)), _whole(), _whole()],
        out_specs=[oblk, oblk],
        out_shape=[jax.ShapeDtypeStruct((nb, N_ACT, tb), jnp.int32), jax.ShapeDtypeStruct((nb, N_ACT, tb), F32)],
        compiler_params=_cparams(("parallel",)),
        name="topk",
    )(qp, k1, k2)


def _pack_table(w):
    wb = w.astype(BF16)
    half = w.shape[1] // 2
    lo = lax.bitcast_convert_type(wb[:, :half], jnp.uint16).astype(jnp.uint32)
    hi = lax.bitcast_convert_type(wb[:, half:], jnp.uint16).astype(jnp.uint32)
    pk = lax.bitcast_convert_type(lo | (hi << 16), jnp.int32)
    return pk.reshape(w.shape[0] * ROWS_PER_EXPERT, LANES)


GATHER_BUFS = 16
SPLIT_ROWS = 8
NT_DIMS = (((1,), (1,)), ((), ()))


def _group_index_layout(e_tok, grp):
    nb, tb, n_act = e_tok.shape
    seg = n_act // grp
    x = e_tok.reshape(nb, tb // grp, grp, grp, seg)
    return jnp.swapaxes(x, 2, 3).reshape(nb, tb, n_act)


GATHER_CHUNKS = 4
PEER_U_SPAN = 0.85
PEER_V_SPAN = 0.6


def _gather_rows(idx_ref, tab_ref, g_ref, g0, k, grp, chunk=None):
    seg = N_ACT // grp
    rows = [idx_ref.at[g0 + kk] for kk in range(grp)]
    per = seg // GATHER_CHUNKS
    jos = range(seg) if chunk is None else range(chunk * per, (chunk + 1) * per)
    for jo in jos:
        for kk in range(grp):
            j = kk * seg + jo
            e4 = pl.multiple_of(rows[kk][k * seg + jo], ROWS_PER_EXPERT)
            g_ref[ROWS_PER_EXPERT * j:ROWS_PER_EXPERT * (j + 1), :] = tab_ref[pl.ds(e4, ROWS_PER_EXPERT), :]


def _token_pipeline(tb, idx_ref, idx_next_ref, tab_ref, compute, bufs, compute_span):
    grp = len(bufs) // 2
    sets = (bufs[:grp], bufs[grp:])
    n_phase = tb // grp
    assert n_phase % 2 == 0 and n_phase >= 2

    @pl.when(pl.program_id(0) == 0)
    def _():
        for k in range(grp):
            _gather_rows(idx_ref, tab_ref, sets[0][k], 0, k, grp)

    def phase(src, dst, t0, next_idx_ref, next_g0):
        steps = [step for k in range(grp) for step in [compute(t0 + k, src[k])] * ROWS_PER_EXPERT]
        n_gather = grp * GATHER_CHUNKS
        done = 0
        for i in range(n_gather):
            while done < len(steps) and int(done * compute_span * n_gather / len(steps)) <= i:
                next(steps[done], None)
                done += 1
            _gather_rows(next_idx_ref, tab_ref, dst[i // GATHER_CHUNKS], next_g0, i // GATHER_CHUNKS, grp,
                         i % GATHER_CHUNKS)
        for gen in steps[done:]:
            next(gen, None)

    def body(it, c):
        t0 = it * grp
        for parity in range(2):
            @pl.when(it % 2 == parity)
            def _():
                phase(sets[parity], sets[1 - parity], t0, idx_ref, t0 + grp)
        return c

    lax.fori_loop(0, n_phase - 1, body, 0)
    phase(sets[1], sets[0], tb - grp, idx_next_ref, 0)


def _index_specs(nb, tb):
    return [pl.BlockSpec((None, tb, N_ACT), lambda i: (i, 0, 0), memory_space=pltpu.SMEM),
            pl.BlockSpec((None, tb, N_ACT), lambda i: (jnp.minimum(i + 1, nb - 1), 0, 0), memory_space=pltpu.SMEM)]


def _expert_weights(g_ref, s):
    return pltpu.bitcast(g_ref[pl.ds(s, N_ACT, stride=ROWS_PER_EXPERT), :], BF16)


def _split_rows(ref, tb, rows):
    @pl.when(pl.program_id(0) == 0)
    def _():
        ref[...] = jnp.zeros(ref.shape, ref.dtype)

    for r, val in enumerate(rows):
        for c in range(ref.shape[0]):
            ref.at[c][pl.ds(r, tb, stride=SPLIT_ROWS), :] = val[:, c * LANES:(c + 1) * LANES]


def _merge_rows(ref, tb, r):
    return jnp.concatenate([ref.at[c][pl.ds(r, tb, stride=SPLIT_ROWS), :] for c in range(ref.shape[0])], axis=1)


def _load_tile(ref, t8):
    return jnp.concatenate([ref.at[c][pl.ds(t8, SPLIT_ROWS), :] for c in range(ref.shape[0])], axis=1)


def _store_tile(ref, t8, val):
    for c in range(ref.shape[0]):
        ref.at[c][pl.ds(t8, SPLIT_ROWS), :] = val[:, c * LANES:(c + 1) * LANES]


def _tile_scratch(tb, width):
    return pltpu.VMEM((width // LANES, tb * SPLIT_ROWS, LANES), F32)


def _hi_lo(x):
    hi = x.astype(BF16).astype(F32)
    return hi, x - hi


def _peer_u_body(idx_ref, idx_next_ref, h_ref, gate_ref, tab_ref, act_ref, *scratch, tb):
    bufs, (r_ref,) = scratch[:GATHER_BUFS], scratch[GATHER_BUFS:]
    half = D_MODEL // 2
    ones = jnp.ones((SPLIT_ROWS, 2 * LANES), BF16)

    def compute(t, g_ref):
        t8 = pl.multiple_of(t * SPLIT_ROWS, SPLIT_ROWS)
        acc = jnp.zeros((N_ACT, LANES), F32)
        hrow = h_ref[pl.ds(t, 1), :]
        for s in range(ROWS_PER_EXPERT):
            w = g_ref[pl.ds(s, N_ACT, stride=ROWS_PER_EXPERT), :]
            lo = pltpu.bitcast(w << 16, F32)
            hi = pltpu.bitcast(w & jnp.int32(-65536), F32)
            acc = (acc + lo * hrow[:, s * LANES:(s + 1) * LANES]
                   + hi * hrow[:, half + s * LANES:half + (s + 1) * LANES])
            if s == ROWS_PER_EXPERT - 1:
                a_hi, a_lo = _hi_lo(acc)
                terms = jnp.concatenate([a_hi.astype(BF16), a_lo.astype(BF16)], axis=1)
                r_ref[0, pl.ds(t8, SPLIT_ROWS), :] = lax.dot_general(ones, terms, NT_DIMS,
                                                                     preferred_element_type=F32)
            yield

    _token_pipeline(tb, idx_ref, idx_next_ref, tab_ref, compute, bufs, compute_span=PEER_U_SPAN)
    a = _merge_rows(r_ref, tb, 0)
    act_ref[...] = _gelu(a) * jnp.transpose(gate_ref[...])


def _peer_u(idx, h2, gates_t, upk, tb):
    nb = idx.shape[0]
    return pl.pallas_call(
        functools.partial(_peer_u_body, tb=tb),
        grid=(nb,),
        in_specs=_index_specs(nb, tb) + [pl.BlockSpec((tb, D_MODEL), lambda i: (i, 0)),
                                        pl.BlockSpec((None, N_ACT, tb), lambda i: (i, 0, 0)), _whole()],
        out_specs=pl.BlockSpec((None, tb, N_ACT), lambda i: (i, 0, 0)),
        out_shape=jax.ShapeDtypeStruct((nb, tb, N_ACT), F32),
        scratch_shapes=[pltpu.VMEM((N_ACT * ROWS_PER_EXPERT, LANES), jnp.int32)] * GATHER_BUFS
        + [_tile_scratch(tb, N_ACT)],
        compiler_params=_cparams(("arbitrary",)),
        name="peer_u",
    )(idx, idx, h2, gates_t, upk)


def _peer_v_body(idx_ref, idx_next_ref, act_ref, x1_ref, tab_ref, out_ref, *scratch, tb):
    bufs, (l_ref, o_ref) = scratch[:GATHER_BUFS], scratch[GATHER_BUFS:]
    ahi, alo = _hi_lo(act_ref[...])
    ahi, alo = ahi.astype(BF16), alo.astype(BF16)
    j2 = 2 * lax.broadcasted_iota(jnp.int32, (N_ACT, 2 * N_ACT), 0)
    col = lax.broadcasted_iota(jnp.int32, (N_ACT, 2 * N_ACT), 1)
    spread = [jnp.where(col == j2 + b, 1.0, 0.0).astype(BF16) for b in range(2)]
    _split_rows(l_ref, tb, [jnp.dot(a, e, preferred_element_type=F32) for e in spread for a in (ahi, alo)])

    def compute(t, g_ref):
        t8 = pl.multiple_of(t * SPLIT_ROWS, SPLIT_ROWS)
        lhs = _load_tile(l_ref, t8).astype(BF16)
        for s in range(ROWS_PER_EXPERT):
            o_ref.at[s][pl.ds(t8, SPLIT_ROWS), :] = jnp.dot(lhs, _expert_weights(g_ref, s),
                                                             preferred_element_type=F32)
            yield

    _token_pipeline(tb, idx_ref, idx_next_ref, tab_ref, compute, bufs, compute_span=PEER_V_SPAN)
    o = [_merge_rows(o_ref, tb, k) for k in range(4)]
    out_ref[...] = x1_ref[...] + jnp.concatenate([o[0] + o[1], o[2] + o[3]], axis=1)


def _peer_v(idx, act, x1, vpk, tb):
    nb = idx.shape[0]
    return pl.pallas_call(
        functools.partial(_peer_v_body, tb=tb),
        grid=(nb,),
        in_specs=_index_specs(nb, tb) + [pl.BlockSpec((None, tb, N_ACT), lambda i: (i, 0, 0)),
                                        pl.BlockSpec((tb, D_MODEL), lambda i: (i, 0)), _whole()],
        out_specs=pl.BlockSpec((tb, D_MODEL), lambda i: (i, 0)),
        out_shape=jax.ShapeDtypeStruct(x1.shape, F32),
        scratch_shapes=[pltpu.VMEM((N_ACT * ROWS_PER_EXPERT, LANES), jnp.int32)] * GATHER_BUFS
        + [_tile_scratch(tb, 2 * N_ACT), _tile_scratch(tb, D_MODEL // 2)],
        compiler_params=_cparams(("arbitrary",)),
        name="peer_v",
    )(idx, idx, act, x1, vpk)


def _tile(n, pref):
    t = min(pref, n)
    while n % t:
        t //= 2
    return t


def kernel(x, mem, norm1_g, w_in, b_gate, b_forget, fox_q_g, fox_k_g, swa_q_g, swa_k_g, swa_sinks, mem_norm_g, w_mem_kv, mem_q_g, mem_k_g, w_fox_o, w_swa_o, w_mem_o, w_out, norm2_g, w_peer_q, peer_keys1, peer_keys2, peer_u, peer_v):
    b, s, d = x.shape
    n = b * s
    assert d == D_MODEL and s % 256 == 0 and mem.shape[1] == N_MEM
    depth = norm1_g.shape[0]
    xf = x.reshape(n, d)
    memf = mem.reshape(b * N_MEM, d)
    tb = BLOCK
    for l in range(depth):
        w = w_in[l]
        o_ff, o_sq, o_sk, o_sv, o_mq, o_gl = 1536, 1544, 2056, 2184, 2312, 2824
        w_att = jnp.concatenate(
            [w[:, 0:o_ff], w[:, o_sq:o_gl], w[:, o_ff:o_sq], jnp.zeros((d, LANES - FOX_HEADS), F32)],
            axis=1).astype(BF16)
        w_gate = w[:, o_gl:].astype(BF16)
        tile2 = lambda g: jnp.concatenate([g, g])
        gains = jnp.stack([tile2(fox_q_g[l]) * HEAD_D ** -0.5, tile2(fox_k_g[l]),
                           tile2(swa_q_g[l]) * HEAD_D ** -0.5, tile2(swa_k_g[l]),
                           mem_q_g[l] * MEM_HD ** -0.5] + [jnp.zeros((LANES,), F32)] * 3)
        bf = jnp.concatenate([b_forget[l], jnp.zeros((LANES - FOX_HEADS,), F32)]).reshape(1, LANES)
        g1 = norm1_g[l].reshape(1, d)

        fq, fk, fv, sq, sk, sv, mq, lf = _inproj(xf, g1, w_att, gains, bf, _tile(n, 512))
        crow = _cumsum(lf.reshape(b, s, LANES))
        o_fox = _fox(fq, fk, fv, crow.reshape(b * FOX_HEADS, 1, s), b, s, 512)
        o_swa = _swa(swa_sinks[l], sq, sk, sv, b, s)
        mk, mv = _memkv(memf, mem_norm_g[l].reshape(1, d), w_mem_kv[l].astype(BF16), mem_k_g[l].reshape(1, LANES))
        o_mem = _mem(mq, mk, mv, b, s, 256)

        x1, h2, qp = _merge(xf, o_fox, o_swa, o_mem, g1, w_gate, b_gate[l].reshape(1, 3 * d),
                            w_fox_o[l].astype(BF16), w_swa_o[l].astype(BF16), w_mem_o[l].astype(BF16),
                            w_out[l].astype(BF16), norm2_g[l].reshape(1, d), w_peer_q[l].astype(BF16), _tile(n, 256))

        e_t, gates_t = _topk(qp, peer_keys1[l].astype(BF16), peer_keys2[l].astype(BF16), tb)
        idx = _group_index_layout(jnp.swapaxes(e_t, 1, 2) * ROWS_PER_EXPERT, GATHER_BUFS // 2)
        act_t = _peer_u(idx, h2, gates_t, _pack_table(peer_u[l]), tb)
        xf = _peer_v(idx, act_t, x1, _pack_table(peer_v[l]), tb)
    return xf.reshape(b, s, d)
```

```python
import functools

import jax
import jax.numpy as jnp
from jax import lax
from jax.experimental import pallas as pl
from jax.experimental.pallas import tpu as pltpu

F32 = jnp.float32
BF16 = jnp.bfloat16

D_MODEL = 1024
BLOCK = 128
FOX_HEADS = 8
HEAD_D = 64
SWA_HEADS = 8
SWA_KV_HEADS = 2
MEM_HEADS = 4
MEM_HD = 128
N_MEM = 256
PEER_HEADS = 8
N_KEYS = 128
N_EXPERTS = N_KEYS * N_KEYS
PEER_TOPK = 16
N_ACT = PEER_HEADS * PEER_TOPK
EPS = 1e-6
NEG_INF = -1e30

LANES = 128
ROWS_PER_EXPERT = 4
VMEM_LIMIT = 48 * 1024 * 1024

C_FQ, C_FK, C_FV, C_SQ, C_SK, C_SV, C_MQ, C_FF = 0, 512, 1024, 1536, 2048, 2176, 2304, 2816
ATT_COLS = 2944


def _cparams(sem):
    return pltpu.CompilerParams(dimension_semantics=sem, vmem_limit_bytes=VMEM_LIMIT)


def _whole():
    return pl.BlockSpec(memory_space=pltpu.VMEM)


def _rms(xf, g):
    return xf * lax.rsqrt(jnp.mean(xf * xf, axis=-1, keepdims=True) + EPS) * g


def _seg_matrix(seg):
    r = (lax.broadcasted_iota(jnp.int32, (2 * LANES, LANES), 0) % LANES) // seg
    c = lax.broadcasted_iota(jnp.int32, (2 * LANES, LANES), 1) // seg
    return jnp.where(r == c, 1.0 / seg, 0.0).astype(BF16)


def _seg_norm(z, seg_mat, gain):
    sq = z * z
    hi = sq.astype(BF16)
    lo = (sq - hi.astype(F32)).astype(BF16)
    ms = jnp.dot(jnp.concatenate([hi, lo], axis=1), seg_mat, preferred_element_type=F32)
    return z * lax.rsqrt(ms + EPS) * gain


def _log_sigmoid(x):
    return jnp.minimum(x, 0.0) - jnp.log1p(jnp.exp(-jnp.abs(x)))


def _sigmoid(x):
    return 1.0 / (1.0 + jnp.exp(-x))


def _gelu(x):
    return 0.5 * x * (1.0 + lax.erf(x * 0.7071067811865476))


def _inproj_kernel(x_ref, g1_ref, w_ref, gains_ref, bf_ref,
                   fq_ref, fk_ref, fv_ref, sq_ref, sk_ref, sv_ref, mq_ref, lf_ref):
    h = _rms(x_ref[...], g1_ref[...]).astype(BF16)
    seg64 = _seg_matrix(HEAD_D)
    seg128 = _seg_matrix(MEM_HD)

    blocks = []
    for out_ref, nblk, seg_mat, gain_row in ((fq_ref, 4, seg64, 0), (fk_ref, 4, seg64, 1), (fv_ref, 4, None, None),
                                            (sq_ref, 4, seg64, 2), (sk_ref, 1, seg64, 3), (sv_ref, 1, None, None),
                                            (mq_ref, 4, seg128, 4)):
        blocks += [(out_ref, b, seg_mat, gain_row) for b in range(nblk)]
    blocks.append((lf_ref, 0, None, None))
    assert len(blocks) * LANES == ATT_COLS

    def proj(c0):
        width = min(2 * LANES, ATT_COLS - c0)
        return jnp.dot(h, w_ref[:, c0:c0 + width], preferred_element_type=F32)

    def finish(z, blk):
        out_ref, b, seg_mat, gain_row = blk
        if out_ref is lf_ref:
            out_ref[...] = _log_sigmoid(z + bf_ref[...])
        elif seg_mat is None:
            out_ref[:, b * LANES:(b + 1) * LANES] = z.astype(BF16)
        else:
            gain = gains_ref[gain_row:gain_row + 1, :]
            out_ref[:, b * LANES:(b + 1) * LANES] = _seg_norm(z, seg_mat, gain).astype(BF16)

    starts = list(range(0, ATT_COLS, 2 * LANES))
    z = proj(starts[0])
    for i, c0 in enumerate(starts):
        z_next = proj(starts[i + 1]) if i + 1 < len(starts) else None
        for half in range(z.shape[1] // LANES):
            finish(z[:, half * LANES:(half + 1) * LANES], blocks[c0 // LANES + half])
        z = z_next


def _inproj(xf, g1, w_att, gains, bf, tm):
    n = xf.shape[0]
    row = lambda w: pl.BlockSpec((tm, w), lambda i: (i, 0))
    out_shape = [jax.ShapeDtypeStruct((n, w), BF16) for w in (512, 512, 512, 512, 128, 128, 512)]
    out_shape.append(jax.ShapeDtypeStruct((n, 128), F32))
    return pl.pallas_call(
        _inproj_kernel,
        grid=(n // tm,),
        in_specs=[row(D_MODEL), _whole(), _whole(), _whole(), _whole()],
        out_specs=[row(512), row(512), row(512), row(512), row(128), row(128), row(512), row(128)],
        out_shape=out_shape,
        compiler_params=_cparams(("parallel",)),
        name="inproj",
    )(xf, g1, w_att, gains, bf)


def _cumsum_kernel(lf_ref, crow_ref, ccol_ref, *, blk):
    s = lf_ref.shape[0]
    r = lax.broadcasted_iota(jnp.int32, (blk, blk), 0)
    c = lax.broadcasted_iota(jnp.int32, (blk, blk), 1)
    tri = jnp.where(c <= r, 1.0, 0.0).astype(F32)
    carry = jnp.zeros((1, LANES), F32)
    for b in range(s // blk):
        x = lf_ref[b * blk:(b + 1) * blk, :]
        cs = jnp.dot(tri, x, preferred_element_type=F32, precision=lax.Precision.HIGHEST) + carry
        ccol_ref[b * blk:(b + 1) * blk, :] = cs
        carry = cs[blk - 1:blk, :]
    ct = jnp.transpose(ccol_ref[...])
    crow_ref[...] = ct[0:FOX_HEADS, :]


def _cumsum(lf3):
    b, s, _ = lf3.shape
    return pl.pallas_call(
        functools.partial(_cumsum_kernel, blk=256),
        grid=(b,),
        in_specs=[pl.BlockSpec((None, s, LANES), lambda i: (i, 0, 0))],
        out_specs=pl.BlockSpec((None, FOX_HEADS, s), lambda i: (i, 0, 0)),
        out_shape=jax.ShapeDtypeStruct((b, FOX_HEADS, s), F32),
        scratch_shapes=[pltpu.VMEM((s, LANES), F32)],
        compiler_params=_cparams(("parallel",)),
        name="cumsum",
    )(lf3)


def _fox_kernel(q_ref, k_ref, v_ref, crow0_ref, crow1_ref, o_ref, m_scr, l_scr, acc_scr, *, tq):
    qi = pl.program_id(2)
    lane = lax.broadcasted_iota(jnp.int32, (tq, LANES), 1)
    q = q_ref[...]
    zero = jnp.zeros_like(q)
    qh = (jnp.where(lane < HEAD_D, q, zero), jnp.where(lane >= HEAD_D, q, zero))
    crow = (crow0_ref, crow1_ref)
    m_scr[...] = jnp.full(m_scr.shape, NEG_INF, F32)
    l_scr[...] = jnp.zeros(l_scr.shape, F32)
    acc_scr[...] = jnp.zeros(acc_scr.shape, F32)
    rows = lax.broadcasted_iota(jnp.int32, (tq, tq), 0)
    cols = lax.broadcasted_iota(jnp.int32, (tq, tq), 1)

    def step(j, masked):
        ks = pl.multiple_of(j * tq, tq)
        kb = k_ref[pl.ds(ks, tq), :]
        vb = v_ref[pl.ds(ks, tq), :]
        scores = [lax.dot_general(qh[a], kb, (((1,), (1,)), ((), ())), preferred_element_type=F32)
                  for a in range(2)]
        probs, alphas = [], []
        for a in range(2):
            s = scores[a] - crow[a][:, pl.ds(ks, tq)]
            if masked:
                s = jnp.where(cols <= rows, s, NEG_INF)
            m_prev = m_scr[a]
            m_next = jnp.maximum(m_prev, jnp.max(s, axis=1, keepdims=True))
            p = jnp.exp(s - jnp.concatenate([m_next] * (tq // LANES), axis=1))
            alpha = jnp.exp(m_prev - m_next)
            l_scr[a] = alpha * l_scr[a] + jnp.sum(p, axis=1, keepdims=True)
            m_scr[a] = m_next
            probs.append(p.astype(BF16))
            alphas.append(alpha)
        for a in range(2):
            acc_scr[a] = alphas[a] * acc_scr[a] + jnp.dot(probs[a], vb, preferred_element_type=F32)

    def body(j, c):
        step(j, False)
        return c

    lax.fori_loop(0, qi, body, 0)
    step(qi, True)
    o0 = acc_scr[0] / l_scr[0]
    o1 = acc_scr[1] / l_scr[1]
    o_ref[...] = jnp.where(lane < HEAD_D, o0, o1).astype(BF16)


def _fox(fq, fk, fv, crow, b, s, tq):
    nq = s // tq
    qspec = pl.BlockSpec((tq, LANES), lambda bi, hp, qi: (bi * nq + qi, hp))
    kvspec = pl.BlockSpec((s, LANES), lambda bi, hp, qi: (bi, hp))
    return pl.pallas_call(
        functools.partial(_fox_kernel, tq=tq),
        grid=(b, FOX_HEADS // 2, nq),
        in_specs=[qspec, kvspec, kvspec,
                  pl.BlockSpec((None, 1, s), lambda bi, hp, qi: (bi * FOX_HEADS + 2 * hp, 0, 0)),
                  pl.BlockSpec((None, 1, s), lambda bi, hp, qi: (bi * FOX_HEADS + 2 * hp + 1, 0, 0))],
        out_specs=qspec,
        out_shape=jax.ShapeDtypeStruct(fq.shape, BF16),
        scratch_shapes=[pltpu.VMEM((2, tq, LANES), F32), pltpu.VMEM((2, tq, LANES), F32),
                        pltpu.VMEM((2, tq, LANES), F32)],
        compiler_params=_cparams(("parallel", "parallel", "arbitrary")),
        name="fox",
    )(fq, fk, fv, crow, crow)


def _swa_kernel(sink_ref, q_ref, kp_ref, kc_ref, vp_ref, vc_ref, o_ref):
    blk = pl.program_id(1)
    lane = lax.broadcasted_iota(jnp.int32, (BLOCK, LANES), 1)
    kb = jnp.concatenate([kp_ref[...], kc_ref[...]], axis=0)
    vb = jnp.concatenate([vp_ref[...], vc_ref[...]], axis=0)
    qi = lax.broadcasted_iota(jnp.int32, (BLOCK, 2 * BLOCK), 0)
    kj = lax.broadcasted_iota(jnp.int32, (BLOCK, 2 * BLOCK), 1)
    dist = qi + BLOCK - kj
    ok = (dist >= 0) & (dist < BLOCK) & ((blk > 0) | (kj >= BLOCK))
    distf = dist.astype(F32)
    rep = SWA_HEADS // SWA_KV_HEADS
    scores = []
    for h in range(SWA_HEADS):
        q128 = q_ref[:, (h // 2) * LANES:(h // 2 + 1) * LANES]
        grp = h // rep
        qa = q128 if h % 2 == grp else pltpu.roll(q128, HEAD_D, axis=1)
        in_grp = (lane >= grp * HEAD_D) & (lane < (grp + 1) * HEAD_D)
        qa = jnp.where(in_grp, qa, jnp.zeros_like(qa))
        scores.append(lax.dot_general(qa, kb, (((1,), (1,)), ((), ())), preferred_element_type=F32))
    probs, dens = [], []
    for h in range(SWA_HEADS):
        s = jnp.where(ok, scores[h] - (2.0 ** -(h + 1)) * distf, NEG_INF)
        sink = sink_ref[h]
        m = jnp.maximum(jnp.max(s, axis=1, keepdims=True), sink)
        p = jnp.exp(s - m)
        dens.append(jnp.sum(p, axis=1, keepdims=True) + jnp.exp(sink - m))
        probs.append(p.astype(BF16))
    outs = []
    for h in range(SWA_HEADS):
        o = jnp.dot(probs[h], vb, preferred_element_type=F32) / dens[h]
        outs.append(o if h % 2 == h // rep else pltpu.roll(o, HEAD_D, axis=1))
    for pair in range(SWA_HEADS // 2):
        o_ref[:, pair * LANES:(pair + 1) * LANES] = jnp.where(lane < HEAD_D, outs[2 * pair],
                                                             outs[2 * pair + 1]).astype(BF16)


def _swa(sinks, sq, sk, sv, b, s):
    nb = s // BLOCK
    cur = lambda bi, i: (bi * nb + i, 0)
    prev = lambda bi, i: (bi * nb + jnp.maximum(i - 1, 0), 0)
    return pl.pallas_call(
        _swa_kernel,
        grid=(b, nb),
        in_specs=[pl.BlockSpec(memory_space=pltpu.SMEM),
                  pl.BlockSpec((BLOCK, 512), cur),
                  pl.BlockSpec((BLOCK, LANES), prev), pl.BlockSpec((BLOCK, LANES), cur),
                  pl.BlockSpec((BLOCK, LANES), prev), pl.BlockSpec((BLOCK, LANES), cur)],
        out_specs=pl.BlockSpec((BLOCK, 512), cur),
        out_shape=jax.ShapeDtypeStruct(sq.shape, BF16),
        compiler_params=_cparams(("parallel", "parallel")),
        name="swa",
    )(sinks, sq, sk, sk, sv, sv)


def _memkv_kernel(mem_ref, g_ref, w_ref, gk_ref, mk_ref, mv_ref):
    mn = _rms(mem_ref[...], g_ref[...]).astype(BF16)
    seg128 = _seg_matrix(MEM_HD)
    width = MEM_HEADS * MEM_HD
    for hd in range(MEM_HEADS):
        z = jnp.dot(mn, w_ref[:, hd * LANES:(hd + 1) * LANES], preferred_element_type=F32)
        mk_ref[:, hd * LANES:(hd + 1) * LANES] = _seg_norm(z, seg128, gk_ref[...]).astype(BF16)
    mv_ref[...] = jnp.dot(mn, w_ref[:, width:2 * width], preferred_element_type=F32).astype(BF16)


def _memkv(memf, g, w, gk):
    n = memf.shape[0]
    width = MEM_HEADS * MEM_HD
    tm = N_MEM
    return pl.pallas_call(
        _memkv_kernel,
        grid=(n // tm,),
        in_specs=[pl.BlockSpec((tm, D_MODEL), lambda i: (i, 0)), _whole(), _whole(), _whole()],
        out_specs=[pl.BlockSpec((tm, width), lambda i: (i, 0)), pl.BlockSpec((tm, width), lambda i: (i, 0))],
        out_shape=[jax.ShapeDtypeStruct((n, width), BF16), jax.ShapeDtypeStruct((n, width), BF16)],
        compiler_params=_cparams(("parallel",)),
        name="memkv",
    )(memf, g, w, gk)


def _mem_kernel(q_ref, k_ref, v_ref, o_ref):
    heads = [slice(hd * LANES, (hd + 1) * LANES) for hd in range(MEM_HEADS)]
    scores = [lax.dot_general(q_ref[:, sl], k_ref[:, sl], (((1,), (1,)), ((), ())), preferred_element_type=F32)
              for sl in heads]
    probs, dens = [], []
    for s in scores:
        p = jnp.exp(s - jnp.max(s, axis=1, keepdims=True))
        dens.append(jnp.sum(p, axis=1, keepdims=True))
        probs.append(p.astype(BF16))
    for sl, p, den in zip(heads, probs, dens):
        o_ref[:, sl] = (jnp.dot(p, v_ref[:, sl], preferred_element_type=F32) / den).astype(BF16)


def _mem(mq, mk, mv, b, s, tq):
    nq = s // tq
    width = MEM_HEADS * MEM_HD
    qspec = pl.BlockSpec((tq, width), lambda bi, qi: (bi * nq + qi, 0))
    kvspec = pl.BlockSpec((N_MEM, width), lambda bi, qi: (bi, 0))
    return pl.pallas_call(
        _mem_kernel,
        grid=(b, nq),
        in_specs=[qspec, kvspec, kvspec],
        out_specs=qspec,
        out_shape=jax.ShapeDtypeStruct(mq.shape, BF16),
        compiler_params=_cparams(("parallel", "parallel")),
        name="mem",
    )(mq, mk, mv)


def _merge_steps(x_ref, of_ref, os_ref, om_ref, g1_ref, wg_ref, bg_ref, wfo_ref, wso_ref, wmo_ref, wout_ref,
                 g2_ref, wpq_ref, x1_ref, h2_ref, qp_out):
    x = x_ref[...]
    h = _rms(x, g1_ref[...]).astype(BF16)
    merged = jnp.zeros(x.shape, F32)
    for br, (o_ref, w_ref) in enumerate(((of_ref, wfo_ref), (os_ref, wso_ref), (om_ref, wmo_ref))):
        sl = slice(br * D_MODEL, (br + 1) * D_MODEL)
        gate = _sigmoid(jnp.dot(h, wg_ref[:, sl], preferred_element_type=F32) + bg_ref[:, sl])
        merged = merged + gate * jnp.dot(o_ref[...], w_ref[...], preferred_element_type=F32)
        yield
    x1 = x + jnp.dot(merged.astype(BF16), wout_ref[...], preferred_element_type=F32)
    x1_ref[...] = x1
    h2 = _rms(x1, g2_ref[...])
    h2_ref[...] = h2
    yield
    qp_out.append(jnp.dot(h2.astype(BF16), wpq_ref[...], preferred_element_type=F32).astype(BF16))
    yield


def _topk_rows(s, k, order, payload=None):
    big = jnp.float32(2 ** 30)
    vals, sels = [], []
    for _ in range(k):
        m = jnp.max(s, axis=0, keepdims=True)
        sel = jnp.min(jnp.where(s == m, order, big), axis=0, keepdims=True)
        hit = order == sel
        vals.append(m)
        if payload is None:
            sels.append(sel)
        else:
            sels.append(jnp.max(jnp.where(hit, payload, -1.0), axis=0, keepdims=True))
        s = jnp.where(hit, -jnp.inf, s)
    return jnp.concatenate(vals, axis=0), jnp.concatenate(sels, axis=0)


def _topk_keys(s, k):
    depth = 4
    width = s.shape[0] // depth
    rid = lax.broadcasted_iota(jnp.int32, (width, s.shape[1]), 0).astype(F32)
    val = [s[i * width:(i + 1) * width] for i in range(depth)]
    idx = [rid + float(i * width) for i in range(depth)]

    def exchange(a, b, index_ordered):
        swap = val[b] > val[a]
        if not index_ordered:
            swap = swap | ((val[b] == val[a]) & (idx[b] < idx[a]))
        val[a], val[b] = jnp.where(swap, val[b], val[a]), jnp.where(swap, val[a], val[b])
        idx[a], idx[b] = jnp.where(swap, idx[b], idx[a]), jnp.where(swap, idx[a], idx[b])

    exchange(0, 1, True)
    exchange(2, 3, True)
    exchange(0, 2, True)
    exchange(1, 3, True)
    exchange(1, 2, False)

    big = jnp.float32(2 ** 30)
    vals, sels = [], []
    for _ in range(k):
        m = jnp.max(val[0], axis=0, keepdims=True)
        sel = jnp.min(jnp.where(val[0] == m, idx[0], big), axis=0, keepdims=True)
        hit = idx[0] == sel
        vals.append(m)
        sels.append(sel)
        for d in range(depth - 1):
            val[d] = jnp.where(hit, val[d + 1], val[d])
            idx[d] = jnp.where(hit, idx[d + 1], idx[d])
        val[depth - 1] = jnp.where(hit, -jnp.inf, val[depth - 1])
    return jnp.concatenate(vals, axis=0), jnp.concatenate(sels, axis=0)


def _pair_candidates(v1, i1, v2, i2):
    t = v1.shape[1]
    sub = lax.broadcasted_iota(jnp.int32, (8, t), 0)
    low = sub < 4
    b_lo = jnp.where(low, sub, sub - 4)
    v2a, i2a = v2[0:8], i2[0:8]
    v2r, i2r = pltpu.roll(v2a, 4, axis=0), pltpu.roll(i2a, 4, axis=0)
    vals, flat, eidx = [], [], []

    def single(a, v2x, i2x, b0, nvalid):
        vals.append(jnp.where(sub < nvalid, v1[a:a + 1] + v2x, -jnp.inf))
        flat.append((a * PEER_TOPK + b0 + sub).astype(F32))
        eidx.append(i1[a:a + 1] * N_KEYS + i2x)

    def double(a, nvalid_a, nvalid_b):
        va = jnp.where(low, v1[a:a + 1], v1[a + 1:a + 2]) + jnp.where(low, v2a, v2r)
        ok = b_lo < jnp.where(low, nvalid_a, nvalid_b)
        vals.append(jnp.where(ok, va, -jnp.inf))
        flat.append((jnp.where(low, a * PEER_TOPK, (a + 1) * PEER_TOPK) + b_lo).astype(F32))
        eidx.append(jnp.where(low, i1[a:a + 1], i1[a + 1:a + 2]) * N_KEYS + jnp.where(low, i2a, i2r))

    single(0, v2a, i2a, 0, 8)
    single(0, v2[8:16], i2[8:16], 8, 8)
    single(1, v2a, i2a, 0, 8)
    single(2, v2a, i2a, 0, 5)
    single(3, v2a, i2a, 0, 4)
    double(4, 3, 2)
    double(6, 2, 2)
    vals.append(v1[8:16] + v2[0:1])
    flat.append(((8 + sub) * PEER_TOPK).astype(F32))
    eidx.append(i1[8:16] * N_KEYS + i2[0:1])
    cat = lambda xs: jnp.concatenate(xs, axis=0)
    return cat(vals), cat(flat), cat(eidx)


def _topk_steps(qp_ref, k1_ref, k2_ref, e_ref, g_ref, tb):
    nt = (((1,), (1,)), ((), ()))
    for blk in range(qp_ref.shape[0] // tb):
        rows = slice(blk * tb, (blk + 1) * tb)
        for hd in range(PEER_HEADS):
            c0 = hd * 2 * N_KEYS
            s1 = lax.dot_general(k1_ref[hd], qp_ref[rows, c0:c0 + N_KEYS], nt, preferred_element_type=F32)
            s2 = lax.dot_general(k2_ref[hd], qp_ref[rows, c0 + N_KEYS:c0 + 2 * N_KEYS], nt,
                                 preferred_element_type=F32)
            v1, i1 = _topk_keys(s1, PEER_TOPK)
            v2, i2 = _topk_keys(s2, PEER_TOPK)
            cand, flat, eidx = _pair_candidates(v1, i1, v2, i2)
            sv, ex = _topk_rows(cand, PEER_TOPK, flat, payload=eidx)
            p = jnp.exp(sv - sv[0:1, :])
            g_ref[blk, hd * PEER_TOPK:(hd + 1) * PEER_TOPK, :] = p / jnp.sum(p, axis=0, keepdims=True)
            e_ref[blk, hd * PEER_TOPK:(hd + 1) * PEER_TOPK, :] = ex.astype(jnp.int32)
            yield


def _merge_topk_kernel(*refs, tb):
    merge_refs, (k1_ref, k2_ref), (x1_ref, h2_ref, e_ref, g_ref), (qp_scr,) = (
        refs[:13], refs[13:15], refs[15:19], refs[19:])

    @pl.when(pl.program_id(0) == 0)
    def _():
        qp_scr[...] = jnp.zeros(qp_scr.shape, qp_scr.dtype)

    qp_new = []
    merge = _merge_steps(*merge_refs, x1_ref, h2_ref, qp_new)
    topk = _topk_steps(qp_scr, k1_ref, k2_ref, e_ref, g_ref, tb)
    n_merge, n_topk = 5, (qp_scr.shape[0] // tb) * PEER_HEADS
    done = 0
    for i in range(n_topk):
        while done < n_merge and done * n_topk <= i * n_merge:
            next(merge)
            done += 1
        next(topk)
    for _ in range(done, n_merge):
        next(merge)
    qp_scr[...] = qp_new[0]


def _merge_topk(xf, of, os_, om, g1, wg, bg, wfo, wso, wmo, wout, g2, wpq, k1, k2, tm, tb):
    n = xf.shape[0]
    nt = n // tm
    per = tm // tb
    cur = lambda i: (jnp.minimum(i, nt - 1), 0)
    prev = lambda i: (jnp.maximum(i - 1, 0), 0, 0)
    row = lambda w: pl.BlockSpec((tm, w), cur)
    oblk = pl.BlockSpec((per, N_ACT, tb), prev)
    return pl.pallas_call(
        functools.partial(_merge_topk_kernel, tb=tb),
        grid=(nt + 1,),
        in_specs=[row(D_MODEL), row(512), row(512), row(512)] + [_whole()] * 11,
        out_specs=[row(D_MODEL), row(D_MODEL), oblk, oblk],
        out_shape=[jax.ShapeDtypeStruct((n, D_MODEL), F32), jax.ShapeDtypeStruct((n, D_MODEL), F32),
                   jax.ShapeDtypeStruct((n // tb, N_ACT, tb), jnp.int32),
                   jax.ShapeDtypeStruct((n // tb, N_ACT, tb), F32)],
        scratch_shapes=[pltpu.VMEM((tm, wpq.shape[1]), BF16)],
        compiler_params=_cparams(("arbitrary",)),
        name="merge_topk",
    )(xf, of, os_, om, g1, wg, bg, wfo, wso, wmo, wout, g2, wpq, k1, k2)


def _pack_table(w):
    wb = w.astype(BF16)
    half = w.shape[1] // 2
    lo = lax.bitcast_convert_type(wb[:, :half], jnp.uint16).astype(jnp.uint32)
    hi = lax.bitcast_convert_type(wb[:, half:], jnp.uint16).astype(jnp.uint32)
    pk = lax.bitcast_convert_type(lo | (hi << 16), jnp.int32)
    return pk.reshape(w.shape[0] * ROWS_PER_EXPERT, LANES)


GATHER_BUFS = 16
SPLIT_ROWS = 8
NT_DIMS = (((1,), (1,)), ((), ()))


def _group_index_layout(e_tok, grp):
    nb, tb, n_act = e_tok.shape
    seg = n_act // grp
    x = e_tok.reshape(nb, tb // grp, grp, grp, seg)
    return jnp.swapaxes(x, 2, 3).reshape(nb, tb, n_act)


GATHER_CHUNKS = 4
PEER_U_SPAN = 0.85
PEER_V_SPAN = 0.6


def _gather_rows(idx_ref, tab_ref, g_ref, g0, k, grp, chunk=None):
    seg = N_ACT // grp
    rows = [idx_ref.at[g0 + kk] for kk in range(grp)]
    per = seg // GATHER_CHUNKS
    jos = range(seg) if chunk is None else range(chunk * per, (chunk + 1) * per)
    for jo in jos:
        for kk in range(grp):
            j = kk * seg + jo
            e4 = pl.multiple_of(rows[kk][k * seg + jo], ROWS_PER_EXPERT)
            g_ref[ROWS_PER_EXPERT * j:ROWS_PER_EXPERT * (j + 1), :] = tab_ref[pl.ds(e4, ROWS_PER_EXPERT), :]


def _token_pipeline(tb, idx_ref, idx_next_ref, tab_ref, compute, bufs, compute_span):
    grp = len(bufs) // 2
    sets = (bufs[:grp], bufs[grp:])
    n_phase = tb // grp
    assert n_phase % 2 == 0 and n_phase >= 2

    @pl.when(pl.program_id(0) == 0)
    def _():
        for k in range(grp):
            _gather_rows(idx_ref, tab_ref, sets[0][k], 0, k, grp)

    def phase(src, dst, t0, next_idx_ref, next_g0):
        steps = [step for k in range(grp) for step in [compute(t0 + k, src[k])] * ROWS_PER_EXPERT]
        n_gather = grp * GATHER_CHUNKS
        done = 0
        for i in range(n_gather):
            while done < len(steps) and int(done * compute_span * n_gather / len(steps)) <= i:
                next(steps[done], None)
                done += 1
            _gather_rows(next_idx_ref, tab_ref, dst[i // GATHER_CHUNKS], next_g0, i // GATHER_CHUNKS, grp,
                         i % GATHER_CHUNKS)
        for gen in steps[done:]:
            next(gen, None)

    def body(it, c):
        t0 = it * grp
        for parity in range(2):
            @pl.when(it % 2 == parity)
            def _():
                phase(sets[parity], sets[1 - parity], t0, idx_ref, t0 + grp)
        return c

    lax.fori_loop(0, n_phase - 1, body, 0)
    phase(sets[1], sets[0], tb - grp, idx_next_ref, 0)


def _index_specs(nb, tb):
    return [pl.BlockSpec((None, tb, N_ACT), lambda i: (i, 0, 0), memory_space=pltpu.SMEM),
            pl.BlockSpec((None, tb, N_ACT), lambda i: (jnp.minimum(i + 1, nb - 1), 0, 0), memory_space=pltpu.SMEM)]


def _expert_weights(g_ref, s):
    return pltpu.bitcast(g_ref[pl.ds(s, N_ACT, stride=ROWS_PER_EXPERT), :], BF16)


def _split_rows(ref, tb, rows):
    @pl.when(pl.program_id(0) == 0)
    def _():
        ref[...] = jnp.zeros(ref.shape, ref.dtype)

    for r, val in enumerate(rows):
        for c in range(ref.shape[0]):
            ref.at[c][pl.ds(r, tb, stride=SPLIT_ROWS), :] = val[:, c * LANES:(c + 1) * LANES]


def _merge_rows(ref, tb, r):
    return jnp.concatenate([ref.at[c][pl.ds(r, tb, stride=SPLIT_ROWS), :] for c in range(ref.shape[0])], axis=1)


def _load_tile(ref, t8):
    return jnp.concatenate([ref.at[c][pl.ds(t8, SPLIT_ROWS), :] for c in range(ref.shape[0])], axis=1)


def _store_tile(ref, t8, val):
    for c in range(ref.shape[0]):
        ref.at[c][pl.ds(t8, SPLIT_ROWS), :] = val[:, c * LANES:(c + 1) * LANES]


def _tile_scratch(tb, width):
    return pltpu.VMEM((width // LANES, tb * SPLIT_ROWS, LANES), F32)


def _hi_lo(x):
    hi = x.astype(BF16).astype(F32)
    return hi, x - hi


def _peer_u_body(idx_ref, idx_next_ref, h_ref, gate_ref, tab_ref, act_ref, *scratch, tb):
    bufs, (r_ref,) = scratch[:GATHER_BUFS], scratch[GATHER_BUFS:]
    half = D_MODEL // 2
    ones = jnp.ones((SPLIT_ROWS, 2 * LANES), BF16)

    def compute(t, g_ref):
        t8 = pl.multiple_of(t * SPLIT_ROWS, SPLIT_ROWS)
        acc = jnp.zeros((N_ACT, LANES), F32)
        hrow = h_ref[pl.ds(t, 1), :]
        for s in range(ROWS_PER_EXPERT):
            w = g_ref[pl.ds(s, N_ACT, stride=ROWS_PER_EXPERT), :]
            lo = pltpu.bitcast(w << 16, F32)
            hi = pltpu.bitcast(w & jnp.int32(-65536), F32)
            acc = (acc + lo * hrow[:, s * LANES:(s + 1) * LANES]
                   + hi * hrow[:, half + s * LANES:half + (s + 1) * LANES])
            if s == ROWS_PER_EXPERT - 1:
                a_hi, a_lo = _hi_lo(acc)
                terms = jnp.concatenate([a_hi.astype(BF16), a_lo.astype(BF16)], axis=1)
                r_ref[0, pl.ds(t8, SPLIT_ROWS), :] = lax.dot_general(ones, terms, NT_DIMS,
                                                                     preferred_element_type=F32)
            yield

    _token_pipeline(tb, idx_ref, idx_next_ref, tab_ref, compute, bufs, compute_span=PEER_U_SPAN)
    a = _merge_rows(r_ref, tb, 0)
    act_ref[...] = _gelu(a) * jnp.transpose(gate_ref[...])


def _peer_u(idx, h2, gates_t, upk, tb):
    nb = idx.shape[0]
    return pl.pallas_call(
        functools.partial(_peer_u_body, tb=tb),
        grid=(nb,),
        in_specs=_index_specs(nb, tb) + [pl.BlockSpec((tb, D_MODEL), lambda i: (i, 0)),
                                        pl.BlockSpec((None, N_ACT, tb), lambda i: (i, 0, 0)), _whole()],
        out_specs=pl.BlockSpec((None, tb, N_ACT), lambda i: (i, 0, 0)),
        out_shape=jax.ShapeDtypeStruct((nb, tb, N_ACT), F32),
        scratch_shapes=[pltpu.VMEM((N_ACT * ROWS_PER_EXPERT, LANES), jnp.int32)] * GATHER_BUFS
        + [_tile_scratch(tb, N_ACT)],
        compiler_params=_cparams(("arbitrary",)),
        name="peer_u",
    )(idx, idx, h2, gates_t, upk)


def _peer_v_body(idx_ref, idx_next_ref, act_ref, x1_ref, tab_ref, out_ref, *scratch, tb):
    bufs, (l_ref, o_ref) = scratch[:GATHER_BUFS], scratch[GATHER_BUFS:]
    ahi, alo = _hi_lo(act_ref[...])
    ahi, alo = ahi.astype(BF16), alo.astype(BF16)
    j2 = 2 * lax.broadcasted_iota(jnp.int32, (N_ACT, 2 * N_ACT), 0)
    col = lax.broadcasted_iota(jnp.int32, (N_ACT, 2 * N_ACT), 1)
    spread = [jnp.where(col == j2 + b, 1.0, 0.0).astype(BF16) for b in range(2)]
    _split_rows(l_ref, tb, [jnp.dot(a, e, preferred_element_type=F32) for e in spread for a in (ahi, alo)])

    def compute(t, g_ref):
        t8 = pl.multiple_of(t * SPLIT_ROWS, SPLIT_ROWS)
        lhs = _load_tile(l_ref, t8).astype(BF16)
        for s in range(ROWS_PER_EXPERT):
            o_ref.at[s][pl.ds(t8, SPLIT_ROWS), :] = jnp.dot(lhs, _expert_weights(g_ref, s),
                                                             preferred_element_type=F32)
            yield

    _token_pipeline(tb, idx_ref, idx_next_ref, tab_ref, compute, bufs, compute_span=PEER_V_SPAN)
    o = [_merge_rows(o_ref, tb, k) for k in range(4)]
    out_ref[...] = x1_ref[...] + jnp.concatenate([o[0] + o[1], o[2] + o[3]], axis=1)


def _peer_v(idx, act, x1, vpk, tb):
    nb = idx.shape[0]
    return pl.pallas_call(
        functools.partial(_peer_v_body, tb=tb),
        grid=(nb,),
        in_specs=_index_specs(nb, tb) + [pl.BlockSpec((None, tb, N_ACT), lambda i: (i, 0, 0)),
                                        pl.BlockSpec((tb, D_MODEL), lambda i: (i, 0)), _whole()],
        out_specs=pl.BlockSpec((tb, D_MODEL), lambda i: (i, 0)),
        out_shape=jax.ShapeDtypeStruct(x1.shape, F32),
        scratch_shapes=[pltpu.VMEM((N_ACT * ROWS_PER_EXPERT, LANES), jnp.int32)] * GATHER_BUFS
        + [_tile_scratch(tb, 2 * N_ACT), _tile_scratch(tb, D_MODEL // 2)],
        compiler_params=_cparams(("arbitrary",)),
        name="peer_v",
    )(idx, idx, act, x1, vpk)


def _tile(n, pref):
    t = min(pref, n)
    while n % t:
        t //= 2
    return t


def kernel(x, mem, norm1_g, w_in, b_gate, b_forget, fox_q_g, fox_k_g, swa_q_g, swa_k_g, swa_sinks, mem_norm_g, w_mem_kv, mem_q_g, mem_k_g, w_fox_o, w_swa_o, w_mem_o, w_out, norm2_g, w_peer_q, peer_keys1, peer_keys2, peer_u, peer_v):
    b, s, d = x.shape
    n = b * s
    assert d == D_MODEL and s % 256 == 0 and mem.shape[1] == N_MEM
    depth = norm1_g.shape[0]
    xf = x.reshape(n, d)
    memf = mem.reshape(b * N_MEM, d)
    tb = BLOCK
    for l in range(depth):
        w = w_in[l]
        o_ff, o_sq, o_sk, o_sv, o_mq, o_gl = 1536, 1544, 2056, 2184, 2312, 2824
        w_att = jnp.concatenate(
            [w[:, 0:o_ff], w[:, o_sq:o_gl], w[:, o_ff:o_sq], jnp.zeros((d, LANES - FOX_HEADS), F32)],
            axis=1).astype(BF16)
        w_gate = w[:, o_gl:].astype(BF16)
        tile2 = lambda g: jnp.concatenate([g, g])
        gains = jnp.stack([tile2(fox_q_g[l]) * HEAD_D ** -0.5, tile2(fox_k_g[l]),
                           tile2(swa_q_g[l]) * HEAD_D ** -0.5, tile2(swa_k_g[l]),
                           mem_q_g[l] * MEM_HD ** -0.5] + [jnp.zeros((LANES,), F32)] * 3)
        bf = jnp.concatenate([b_forget[l], jnp.zeros((LANES - FOX_HEADS,), F32)]).reshape(1, LANES)
        g1 = norm1_g[l].reshape(1, d)

        fq, fk, fv, sq, sk, sv, mq, lf = _inproj(xf, g1, w_att, gains, bf, _tile(n, 512))
        crow = _cumsum(lf.reshape(b, s, LANES))
        o_fox = _fox(fq, fk, fv, crow.reshape(b * FOX_HEADS, 1, s), b, s, 512)
        o_swa = _swa(swa_sinks[l], sq, sk, sv, b, s)
        mk, mv = _memkv(memf, mem_norm_g[l].reshape(1, d), w_mem_kv[l].astype(BF16), mem_k_g[l].reshape(1, LANES))
        o_mem = _mem(mq, mk, mv, b, s, 256)

        x1, h2, e_t, gates_t = _merge_topk(
            xf, o_fox, o_swa, o_mem, g1, w_gate, b_gate[l].reshape(1, 3 * d),
            w_fox_o[l].astype(BF16), w_swa_o[l].astype(BF16), w_mem_o[l].astype(BF16), w_out[l].astype(BF16),
            norm2_g[l].reshape(1, d), w_peer_q[l].astype(BF16), peer_keys1[l].astype(BF16),
            peer_keys2[l].astype(BF16), _tile(n, 256), tb)
        idx = _group_index_layout(jnp.swapaxes(e_t, 1, 2) * ROWS_PER_EXPERT, GATHER_BUFS // 2)
        act_t = _peer_u(idx, h2, gates_t, _pack_table(peer_u[l]), tb)
        xf = _peer_v(idx, act_t, x1, _pack_table(peer_v[l]), tb)
    return xf.reshape(b, s, d)
```

```python
import functools

import jax
import jax.numpy as jnp
from jax import lax
from jax.experimental import pallas as pl
from jax.experimental.pallas import tpu as pltpu

F32 = jnp.float32
BF16 = jnp.bfloat16

D_MODEL = 1024
BLOCK = 128
FOX_HEADS = 8
HEAD_D = 64
SWA_HEADS = 8
SWA_KV_HEADS = 2
MEM_HEADS = 4
MEM_HD = 128
N_MEM = 256
PEER_HEADS = 8
N_KEYS = 128
N_EXPERTS = N_KEYS * N_KEYS
PEER_TOPK = 16
N_ACT = PEER_HEADS * PEER_TOPK
EPS = 1e-6
NEG_INF = -1e30

LANES = 128
ROWS_PER_EXPERT = 4
VMEM_LIMIT = 48 * 1024 * 1024

C_FQ, C_FK, C_FV, C_SQ, C_SK, C_SV, C_MQ, C_FF = 0, 512, 1024, 1536, 2048, 2176, 2304, 2816
ATT_COLS = 2944


def _cparams(sem):
    return pltpu.CompilerParams(dimension_semantics=sem, vmem_limit_bytes=VMEM_LIMIT)


def _whole():
    return pl.BlockSpec(memory_space=pltpu.VMEM)


def _rms(xf, g):
    return xf * lax.rsqrt(jnp.mean(xf * xf, axis=-1, keepdims=True) + EPS) * g


def _seg_matrix(seg):
    r = (lax.broadcasted_iota(jnp.int32, (2 * LANES, LANES), 0) % LANES) // seg
    c = lax.broadcasted_iota(jnp.int32, (2 * LANES, LANES), 1) // seg
    return jnp.where(r == c, 1.0 / seg, 0.0).astype(BF16)


def _seg_norm(z, seg_mat, gain):
    sq = z * z
    hi = sq.astype(BF16)
    lo = (sq - hi.astype(F32)).astype(BF16)
    ms = jnp.dot(jnp.concatenate([hi, lo], axis=1), seg_mat, preferred_element_type=F32)
    return z * lax.rsqrt(ms + EPS) * gain


def _log_sigmoid(x):
    return jnp.minimum(x, 0.0) - jnp.log1p(jnp.exp(-jnp.abs(x)))


def _sigmoid(x):
    return 1.0 / (1.0 + jnp.exp(-x))


def _gelu(x):
    return 0.5 * x * (1.0 + lax.erf(x * 0.7071067811865476))


def _inproj_kernel(x_ref, g1_ref, w_ref, gains_ref, bf_ref,
                   fq_ref, fk_ref, fv_ref, sq_ref, sk_ref, sv_ref, mq_ref, lf_ref):
    h = _rms(x_ref[...], g1_ref[...]).astype(BF16)
    seg64 = _seg_matrix(HEAD_D)
    seg128 = _seg_matrix(MEM_HD)

    blocks = []
    for out_ref, nblk, seg_mat, gain_row in ((fq_ref, 4, seg64, 0), (fk_ref, 4, seg64, 1), (fv_ref, 4, None, None),
                                            (sq_ref, 4, seg64, 2), (sk_ref, 1, seg64, 3), (sv_ref, 1, None, None),
                                            (mq_ref, 4, seg128, 4)):
        blocks += [(out_ref, b, seg_mat, gain_row) for b in range(nblk)]
    blocks.append((lf_ref, 0, None, None))
    assert len(blocks) * LANES == ATT_COLS

    def proj(c0):
        width = min(2 * LANES, ATT_COLS - c0)
        return jnp.dot(h, w_ref[:, c0:c0 + width], preferred_element_type=F32)

    def finish(z, blk):
        out_ref, b, seg_mat, gain_row = blk
        if out_ref is lf_ref:
            out_ref[...] = _log_sigmoid(z + bf_ref[...])
        elif seg_mat is None:
            out_ref[:, b * LANES:(b + 1) * LANES] = z.astype(BF16)
        else:
            gain = gains_ref[gain_row:gain_row + 1, :]
            out_ref[:, b * LANES:(b + 1) * LANES] = _seg_norm(z, seg_mat, gain).astype(BF16)

    starts = list(range(0, ATT_COLS, 2 * LANES))
    z = proj(starts[0])
    for i, c0 in enumerate(starts):
        z_next = proj(starts[i + 1]) if i + 1 < len(starts) else None
        for half in range(z.shape[1] // LANES):
            finish(z[:, half * LANES:(half + 1) * LANES], blocks[c0 // LANES + half])
        z = z_next


def _inproj(xf, g1, w_att, gains, bf, tm):
    n = xf.shape[0]
    row = lambda w: pl.BlockSpec((tm, w), lambda i: (i, 0))
    out_shape = [jax.ShapeDtypeStruct((n, w), BF16) for w in (512, 512, 512, 512, 128, 128, 512)]
    out_shape.append(jax.ShapeDtypeStruct((n, 128), F32))
    return pl.pallas_call(
        _inproj_kernel,
        grid=(n // tm,),
        in_specs=[row(D_MODEL), _whole(), _whole(), _whole(), _whole()],
        out_specs=[row(512), row(512), row(512), row(512), row(128), row(128), row(512), row(128)],
        out_shape=out_shape,
        compiler_params=_cparams(("parallel",)),
        name="inproj",
    )(xf, g1, w_att, gains, bf)


def _cumsum_kernel(lf_ref, crow_ref, ccol_ref, *, blk):
    s = lf_ref.shape[0]
    r = lax.broadcasted_iota(jnp.int32, (blk, blk), 0)
    c = lax.broadcasted_iota(jnp.int32, (blk, blk), 1)
    tri = jnp.where(c <= r, 1.0, 0.0).astype(F32)
    carry = jnp.zeros((1, LANES), F32)
    for b in range(s // blk):
        x = lf_ref[b * blk:(b + 1) * blk, :]
        cs = jnp.dot(tri, x, preferred_element_type=F32, precision=lax.Precision.HIGHEST) + carry
        ccol_ref[b * blk:(b + 1) * blk, :] = cs
        carry = cs[blk - 1:blk, :]
    ct = jnp.transpose(ccol_ref[...])
    crow_ref[...] = ct[0:FOX_HEADS, :]


def _cumsum(lf3):
    b, s, _ = lf3.shape
    return pl.pallas_call(
        functools.partial(_cumsum_kernel, blk=256),
        grid=(b,),
        in_specs=[pl.BlockSpec((None, s, LANES), lambda i: (i, 0, 0))],
        out_specs=pl.BlockSpec((None, FOX_HEADS, s), lambda i: (i, 0, 0)),
        out_shape=jax.ShapeDtypeStruct((b, FOX_HEADS, s), F32),
        scratch_shapes=[pltpu.VMEM((s, LANES), F32)],
        compiler_params=_cparams(("parallel",)),
        name="cumsum",
    )(lf3)


def _fox_kernel(q_ref, k_ref, v_ref, crow0_ref, crow1_ref, o_ref, m_scr, l_scr, acc_scr, *, tq):
    qi = pl.program_id(2)
    lane = lax.broadcasted_iota(jnp.int32, (tq, LANES), 1)
    q = q_ref[...]
    zero = jnp.zeros_like(q)
    qh = (jnp.where(lane < HEAD_D, q, zero), jnp.where(lane >= HEAD_D, q, zero))
    crow = (crow0_ref, crow1_ref)
    m_scr[...] = jnp.full(m_scr.shape, NEG_INF, F32)
    l_scr[...] = jnp.zeros(l_scr.shape, F32)
    acc_scr[...] = jnp.zeros(acc_scr.shape, F32)
    rows = lax.broadcasted_iota(jnp.int32, (tq, tq), 0)
    cols = lax.broadcasted_iota(jnp.int32, (tq, tq), 1)

    def step(j, masked):
        ks = pl.multiple_of(j * tq, tq)
        kb = k_ref[pl.ds(ks, tq), :]
        vb = v_ref[pl.ds(ks, tq), :]
        scores = [lax.dot_general(qh[a], kb, (((1,), (1,)), ((), ())), preferred_element_type=F32)
                  for a in range(2)]
        probs, alphas = [], []
        for a in range(2):
            s = scores[a] - crow[a][:, pl.ds(ks, tq)]
            if masked:
                s = jnp.where(cols <= rows, s, NEG_INF)
            m_prev = m_scr[a]
            m_next = jnp.maximum(m_prev, jnp.max(s, axis=1, keepdims=True))
            p = jnp.exp(s - jnp.concatenate([m_next] * (tq // LANES), axis=1))
            alpha = jnp.exp(m_prev - m_next)
            l_scr[a] = alpha * l_scr[a] + jnp.sum(p, axis=1, keepdims=True)
            m_scr[a] = m_next
            probs.append(p.astype(BF16))
            alphas.append(alpha)
        for a in range(2):
            acc_scr[a] = alphas[a] * acc_scr[a] + jnp.dot(probs[a], vb, preferred_element_type=F32)

    def body(j, c):
        step(j, False)
        return c

    lax.fori_loop(0, qi, body, 0)
    step(qi, True)
    o0 = acc_scr[0] / l_scr[0]
    o1 = acc_scr[1] / l_scr[1]
    o_ref[...] = jnp.where(lane < HEAD_D, o0, o1).astype(BF16)


def _fox(fq, fk, fv, crow, b, s, tq):
    nq = s // tq
    qspec = pl.BlockSpec((tq, LANES), lambda bi, hp, qi: (bi * nq + qi, hp))
    kvspec = pl.BlockSpec((s, LANES), lambda bi, hp, qi: (bi, hp))
    return pl.pallas_call(
        functools.partial(_fox_kernel, tq=tq),
        grid=(b, FOX_HEADS // 2, nq),
        in_specs=[qspec, kvspec, kvspec,
                  pl.BlockSpec((None, 1, s), lambda bi, hp, qi: (bi * FOX_HEADS + 2 * hp, 0, 0)),
                  pl.BlockSpec((None, 1, s), lambda bi, hp, qi: (bi * FOX_HEADS + 2 * hp + 1, 0, 0))],
        out_specs=qspec,
        out_shape=jax.ShapeDtypeStruct(fq.shape, BF16),
        scratch_shapes=[pltpu.VMEM((2, tq, LANES), F32), pltpu.VMEM((2, tq, LANES), F32),
                        pltpu.VMEM((2, tq, LANES), F32)],
        compiler_params=_cparams(("parallel", "parallel", "arbitrary")),
        name="fox",
    )(fq, fk, fv, crow, crow)


def _swa_kernel(sink_ref, q_ref, kp_ref, kc_ref, vp_ref, vc_ref, o_ref):
    blk = pl.program_id(1)
    lane = lax.broadcasted_iota(jnp.int32, (BLOCK, LANES), 1)
    kb = jnp.concatenate([kp_ref[...], kc_ref[...]], axis=0)
    vb = jnp.concatenate([vp_ref[...], vc_ref[...]], axis=0)
    qi = lax.broadcasted_iota(jnp.int32, (BLOCK, 2 * BLOCK), 0)
    kj = lax.broadcasted_iota(jnp.int32, (BLOCK, 2 * BLOCK), 1)
    dist = qi + BLOCK - kj
    ok = (dist >= 0) & (dist < BLOCK) & ((blk > 0) | (kj >= BLOCK))
    distf = dist.astype(F32)
    rep = SWA_HEADS // SWA_KV_HEADS
    scores = []
    for h in range(SWA_HEADS):
        q128 = q_ref[:, (h // 2) * LANES:(h // 2 + 1) * LANES]
        grp = h // rep
        qa = q128 if h % 2 == grp else pltpu.roll(q128, HEAD_D, axis=1)
        in_grp = (lane >= grp * HEAD_D) & (lane < (grp + 1) * HEAD_D)
        qa = jnp.where(in_grp, qa, jnp.zeros_like(qa))
        scores.append(lax.dot_general(qa, kb, (((1,), (1,)), ((), ())), preferred_element_type=F32))
    probs, dens = [], []
    for h in range(SWA_HEADS):
        s = jnp.where(ok, scores[h] - (2.0 ** -(h + 1)) * distf, NEG_INF)
        sink = sink_ref[h]
        m = jnp.maximum(jnp.max(s, axis=1, keepdims=True), sink)
        p = jnp.exp(s - m)
        dens.append(jnp.sum(p, axis=1, keepdims=True) + jnp.exp(sink - m))
        probs.append(p.astype(BF16))
    outs = []
    for h in range(SWA_HEADS):
        o = jnp.dot(probs[h], vb, preferred_element_type=F32) / dens[h]
        outs.append(o if h % 2 == h // rep else pltpu.roll(o, HEAD_D, axis=1))
    for pair in range(SWA_HEADS // 2):
        o_ref[:, pair * LANES:(pair + 1) * LANES] = jnp.where(lane < HEAD_D, outs[2 * pair],
                                                             outs[2 * pair + 1]).astype(BF16)


def _swa(sinks, sq, sk, sv, b, s):
    nb = s // BLOCK
    cur = lambda bi, i: (bi * nb + i, 0)
    prev = lambda bi, i: (bi * nb + jnp.maximum(i - 1, 0), 0)
    return pl.pallas_call(
        _swa_kernel,
        grid=(b, nb),
        in_specs=[pl.BlockSpec(memory_space=pltpu.SMEM),
                  pl.BlockSpec((BLOCK, 512), cur),
                  pl.BlockSpec((BLOCK, LANES), prev), pl.BlockSpec((BLOCK, LANES), cur),
                  pl.BlockSpec((BLOCK, LANES), prev), pl.BlockSpec((BLOCK, LANES), cur)],
        out_specs=pl.BlockSpec((BLOCK, 512), cur),
        out_shape=jax.ShapeDtypeStruct(sq.shape, BF16),
        compiler_params=_cparams(("parallel", "parallel")),
        name="swa",
    )(sinks, sq, sk, sk, sv, sv)


def _memkv_kernel(mem_ref, g_ref, w_ref, gk_ref, mk_ref, mv_ref):
    mn = _rms(mem_ref[...], g_ref[...]).astype(BF16)
    seg128 = _seg_matrix(MEM_HD)
    width = MEM_HEADS * MEM_HD
    for hd in range(MEM_HEADS):
        z = jnp.dot(mn, w_ref[:, hd * LANES:(hd + 1) * LANES], preferred_element_type=F32)
        mk_ref[:, hd * LANES:(hd + 1) * LANES] = _seg_norm(z, seg128, gk_ref[...]).astype(BF16)
    mv_ref[...] = jnp.dot(mn, w_ref[:, width:2 * width], preferred_element_type=F32).astype(BF16)


def _memkv(memf, g, w, gk):
    n = memf.shape[0]
    width = MEM_HEADS * MEM_HD
    tm = N_MEM
    return pl.pallas_call(
        _memkv_kernel,
        grid=(n // tm,),
        in_specs=[pl.BlockSpec((tm, D_MODEL), lambda i: (i, 0)), _whole(), _whole(), _whole()],
        out_specs=[pl.BlockSpec((tm, width), lambda i: (i, 0)), pl.BlockSpec((tm, width), lambda i: (i, 0))],
        out_shape=[jax.ShapeDtypeStruct((n, width), BF16), jax.ShapeDtypeStruct((n, width), BF16)],
        compiler_params=_cparams(("parallel",)),
        name="memkv",
    )(memf, g, w, gk)


def _mem_kernel(q_ref, k_ref, v_ref, o_ref):
    heads = [slice(hd * LANES, (hd + 1) * LANES) for hd in range(MEM_HEADS)]
    scores = [lax.dot_general(q_ref[:, sl], k_ref[:, sl], (((1,), (1,)), ((), ())), preferred_element_type=F32)
              for sl in heads]
    probs, dens = [], []
    for s in scores:
        p = jnp.exp(s - jnp.max(s, axis=1, keepdims=True))
        dens.append(jnp.sum(p, axis=1, keepdims=True))
        probs.append(p.astype(BF16))
    for sl, p, den in zip(heads, probs, dens):
        o_ref[:, sl] = (jnp.dot(p, v_ref[:, sl], preferred_element_type=F32) / den).astype(BF16)


def _mem(mq, mk, mv, b, s, tq):
    nq = s // tq
    width = MEM_HEADS * MEM_HD
    qspec = pl.BlockSpec((tq, width), lambda bi, qi: (bi * nq + qi, 0))
    kvspec = pl.BlockSpec((N_MEM, width), lambda bi, qi: (bi, 0))
    return pl.pallas_call(
        _mem_kernel,
        grid=(b, nq),
        in_specs=[qspec, kvspec, kvspec],
        out_specs=qspec,
        out_shape=jax.ShapeDtypeStruct(mq.shape, BF16),
        compiler_params=_cparams(("parallel", "parallel")),
        name="mem",
    )(mq, mk, mv)


def _merge_kernel(x_ref, of_ref, os_ref, om_ref, g1_ref, wg_ref, bg_ref, wfo_ref, wso_ref, wmo_ref, wout_ref,
                  g2_ref, wpq_ref, x1_ref, h2_ref, qp_ref):
    x = x_ref[...]
    h = _rms(x, g1_ref[...]).astype(BF16)
    merged = jnp.zeros(x.shape, F32)
    for br, (o_ref, w_ref) in enumerate(((of_ref, wfo_ref), (os_ref, wso_ref), (om_ref, wmo_ref))):
        sl = slice(br * D_MODEL, (br + 1) * D_MODEL)
        gate = _sigmoid(jnp.dot(h, wg_ref[:, sl], preferred_element_type=F32) + bg_ref[:, sl])
        merged = merged + gate * jnp.dot(o_ref[...], w_ref[...], preferred_element_type=F32)
    x1 = x + jnp.dot(merged.astype(BF16), wout_ref[...], preferred_element_type=F32)
    x1_ref[...] = x1
    h2 = _rms(x1, g2_ref[...])
    h2_ref[...] = h2
    qp_ref[...] = jnp.dot(h2.astype(BF16), wpq_ref[...], preferred_element_type=F32).astype(BF16)


def _merge(xf, of, os_, om, g1, wg, bg, wfo, wso, wmo, wout, g2, wpq, tm):
    n = xf.shape[0]
    row = lambda w: pl.BlockSpec((tm, w), lambda i: (i, 0))
    qw = wpq.shape[1]
    return pl.pallas_call(
        _merge_kernel,
        grid=(n // tm,),
        in_specs=[row(D_MODEL), row(512), row(512), row(512)] + [_whole()] * 9,
        out_specs=[row(D_MODEL), row(D_MODEL), row(qw)],
        out_shape=[jax.ShapeDtypeStruct((n, D_MODEL), F32), jax.ShapeDtypeStruct((n, D_MODEL), F32),
                   jax.ShapeDtypeStruct((n, qw), BF16)],
        compiler_params=_cparams(("parallel",)),
        name="merge",
    )(xf, of, os_, om, g1, wg, bg, wfo, wso, wmo, wout, g2, wpq)


def _topk_rows(s, k, order, payload=None):
    big = jnp.float32(2 ** 30)
    vals, sels = [], []
    for _ in range(k):
        m = jnp.max(s, axis=0, keepdims=True)
        sel = jnp.min(jnp.where(s == m, order, big), axis=0, keepdims=True)
        hit = order == sel
        vals.append(m)
        if payload is None:
            sels.append(sel)
        else:
            sels.append(jnp.max(jnp.where(hit, payload, -1.0), axis=0, keepdims=True))
        s = jnp.where(hit, -jnp.inf, s)
    return jnp.concatenate(vals, axis=0), jnp.concatenate(sels, axis=0)


def _topk_keys(s, k):
    depth = 4
    width = s.shape[0] // depth
    rid = lax.broadcasted_iota(jnp.int32, (width, s.shape[1]), 0).astype(F32)
    val = [s[i * width:(i + 1) * width] for i in range(depth)]
    idx = [rid + float(i * width) for i in range(depth)]

    def exchange(a, b, index_ordered):
        swap = val[b] > val[a]
        if not index_ordered:
            swap = swap | ((val[b] == val[a]) & (idx[b] < idx[a]))
        val[a], val[b] = jnp.where(swap, val[b], val[a]), jnp.where(swap, val[a], val[b])
        idx[a], idx[b] = jnp.where(swap, idx[b], idx[a]), jnp.where(swap, idx[a], idx[b])

    exchange(0, 1, True)
    exchange(2, 3, True)
    exchange(0, 2, True)
    exchange(1, 3, True)
    exchange(1, 2, False)

    big = jnp.float32(2 ** 30)
    vals, sels = [], []
    for _ in range(k):
        m = jnp.max(val[0], axis=0, keepdims=True)
        sel = jnp.min(jnp.where(val[0] == m, idx[0], big), axis=0, keepdims=True)
        hit = idx[0] == sel
        vals.append(m)
        sels.append(sel)
        for d in range(depth - 1):
            val[d] = jnp.where(hit, val[d + 1], val[d])
            idx[d] = jnp.where(hit, idx[d + 1], idx[d])
        val[depth - 1] = jnp.where(hit, -jnp.inf, val[depth - 1])
    return jnp.concatenate(vals, axis=0), jnp.concatenate(sels, axis=0)


def _pair_candidates(v1, i1, v2, i2):
    t = v1.shape[1]
    sub = lax.broadcasted_iota(jnp.int32, (8, t), 0)
    low = sub < 4
    b_lo = jnp.where(low, sub, sub - 4)
    v2a, i2a = v2[0:8], i2[0:8]
    v2r, i2r = pltpu.roll(v2a, 4, axis=0), pltpu.roll(i2a, 4, axis=0)
    vals, flat, eidx = [], [], []

    def single(a, v2x, i2x, b0, nvalid):
        vals.append(jnp.where(sub < nvalid, v1[a:a + 1] + v2x, -jnp.inf))
        flat.append((a * PEER_TOPK + b0 + sub).astype(F32))
        eidx.append(i1[a:a + 1] * N_KEYS + i2x)

    def double(a, nvalid_a, nvalid_b):
        va = jnp.where(low, v1[a:a + 1], v1[a + 1:a + 2]) + jnp.where(low, v2a, v2r)
        ok = b_lo < jnp.where(low, nvalid_a, nvalid_b)
        vals.append(jnp.where(ok, va, -jnp.inf))
        flat.append((jnp.where(low, a * PEER_TOPK, (a + 1) * PEER_TOPK) + b_lo).astype(F32))
        eidx.append(jnp.where(low, i1[a:a + 1], i1[a + 1:a + 2]) * N_KEYS + jnp.where(low, i2a, i2r))

    single(0, v2a, i2a, 0, 8)
    single(0, v2[8:16], i2[8:16], 8, 8)
    single(1, v2a, i2a, 0, 8)
    single(2, v2a, i2a, 0, 5)
    single(3, v2a, i2a, 0, 4)
    double(4, 3, 2)
    double(6, 2, 2)
    vals.append(v1[8:16] + v2[0:1])
    flat.append(((8 + sub) * PEER_TOPK).astype(F32))
    eidx.append(i1[8:16] * N_KEYS + i2[0:1])
    cat = lambda xs: jnp.concatenate(xs, axis=0)
    return cat(vals), cat(flat), cat(eidx)


def _topk_kernel(qp_ref, k1_ref, k2_ref, e_ref, g_ref):
    nt = (((1,), (1,)), ((), ()))
    for hd in range(PEER_HEADS):
        c0 = hd * 2 * N_KEYS
        s1 = lax.dot_general(k1_ref[hd], qp_ref[:, c0:c0 + N_KEYS], nt, preferred_element_type=F32)
        s2 = lax.dot_general(k2_ref[hd], qp_ref[:, c0 + N_KEYS:c0 + 2 * N_KEYS], nt, preferred_element_type=F32)
        v1, i1 = _topk_keys(s1, PEER_TOPK)
        v2, i2 = _topk_keys(s2, PEER_TOPK)
        cand, flat, eidx = _pair_candidates(v1, i1, v2, i2)
        sv, ex = _topk_rows(cand, PEER_TOPK, flat, payload=eidx)
        p = jnp.exp(sv - sv[0:1, :])
        g_ref[hd * PEER_TOPK:(hd + 1) * PEER_TOPK, :] = p / jnp.sum(p, axis=0, keepdims=True)
        e_ref[hd * PEER_TOPK:(hd + 1) * PEER_TOPK, :] = ex.astype(jnp.int32)


def _topk(qp, k1, k2, tb):
    n = qp.shape[0]
    nb = n // tb
    oblk = pl.BlockSpec((None, N_ACT, tb), lambda i: (i, 0, 0))
    return pl.pallas_call(
        _topk_kernel,
        grid=(nb,),
        in_specs=[pl.BlockSpec((tb, qp.shape[1]), lambda i: (i, 0)), _whole(), _whole()],
        out_specs=[oblk, oblk],
        out_shape=[jax.ShapeDtypeStruct((nb, N_ACT, tb), jnp.int32), jax.ShapeDtypeStruct((nb, N_ACT, tb), F32)],
        compiler_params=_cparams(("parallel",)),
        name="topk",
    )(qp, k1, k2)


def _pack_kernel(w_ref, o_ref):
    tm = w_ref.shape[0]
    half = w_ref.shape[1] // 2
    for s in range(ROWS_PER_EXPERT):
        lo = w_ref[:, s * LANES:(s + 1) * LANES].astype(BF16).astype(F32)
        hi = w_ref[:, half + s * LANES:half + (s + 1) * LANES].astype(BF16).astype(F32)
        word = lax.shift_right_logical(pltpu.bitcast(lo, jnp.int32), 16) | pltpu.bitcast(hi, jnp.int32)
        o_ref[pl.ds(s, tm, stride=ROWS_PER_EXPERT), :] = word


def _pack_table(w, tm=512):
    e = w.shape[0]
    tm = _tile(e, tm)
    return pl.pallas_call(
        _pack_kernel,
        grid=(e // tm,),
        in_specs=[pl.BlockSpec((tm, w.shape[1]), lambda i: (i, 0))],
        out_specs=pl.BlockSpec((tm * ROWS_PER_EXPERT, LANES), lambda i: (i, 0)),
        out_shape=jax.ShapeDtypeStruct((e * ROWS_PER_EXPERT, LANES), jnp.int32),
        compiler_params=_cparams(("parallel",)),
        name="pack",
    )(w)


GATHER_BUFS = 16
SPLIT_ROWS = 8
NT_DIMS = (((1,), (1,)), ((), ()))


def _group_index_layout(e_tok, grp):
    nb, tb, n_act = e_tok.shape
    seg = n_act // grp
    x = e_tok.reshape(nb, tb // grp, grp, grp, seg)
    return jnp.swapaxes(x, 2, 3).reshape(nb, tb, n_act)


GATHER_CHUNKS = 4
PEER_U_SPAN = 0.85
PEER_V_SPAN = 0.6


def _gather_rows(idx_ref, tab_ref, g_ref, g0, k, grp, chunk=None):
    seg = N_ACT // grp
    rows = [idx_ref.at[g0 + kk] for kk in range(grp)]
    per = seg // GATHER_CHUNKS
    jos = range(seg) if chunk is None else range(chunk * per, (chunk + 1) * per)
    for jo in jos:
        for kk in range(grp):
            j = kk * seg + jo
            e4 = pl.multiple_of(rows[kk][k * seg + jo], ROWS_PER_EXPERT)
            g_ref[ROWS_PER_EXPERT * j:ROWS_PER_EXPERT * (j + 1), :] = tab_ref[pl.ds(e4, ROWS_PER_EXPERT), :]


def _token_pipeline(tb, idx_ref, idx_next_ref, tab_ref, compute, bufs, compute_span):
    grp = len(bufs) // 2
    sets = (bufs[:grp], bufs[grp:])
    n_phase = tb // grp
    assert n_phase % 2 == 0 and n_phase >= 2

    @pl.when(pl.program_id(0) == 0)
    def _():
        for k in range(grp):
            _gather_rows(idx_ref, tab_ref, sets[0][k], 0, k, grp)

    def phase(src, dst, t0, next_idx_ref, next_g0):
        steps = [step for k in range(grp) for step in [compute(t0 + k, src[k])] * ROWS_PER_EXPERT]
        n_gather = grp * GATHER_CHUNKS
        done = 0
        for i in range(n_gather):
            while done < len(steps) and int(done * compute_span * n_gather / len(steps)) <= i:
                next(steps[done], None)
                done += 1
            _gather_rows(next_idx_ref, tab_ref, dst[i // GATHER_CHUNKS], next_g0, i // GATHER_CHUNKS, grp,
                         i % GATHER_CHUNKS)
        for gen in steps[done:]:
            next(gen, None)

    def body(it, c):
        t0 = it * grp
        for parity in range(2):
            @pl.when(it % 2 == parity)
            def _():
                phase(sets[parity], sets[1 - parity], t0, idx_ref, t0 + grp)
        return c

    lax.fori_loop(0, n_phase - 1, body, 0)
    phase(sets[1], sets[0], tb - grp, idx_next_ref, 0)


def _index_specs(nb, tb):
    return [pl.BlockSpec((None, tb, N_ACT), lambda i: (i, 0, 0), memory_space=pltpu.SMEM),
            pl.BlockSpec((None, tb, N_ACT), lambda i: (jnp.minimum(i + 1, nb - 1), 0, 0), memory_space=pltpu.SMEM)]


def _expert_weights(g_ref, s):
    return pltpu.bitcast(g_ref[pl.ds(s, N_ACT, stride=ROWS_PER_EXPERT), :], BF16)


def _split_rows(ref, tb, rows):
    @pl.when(pl.program_id(0) == 0)
    def _():
        ref[...] = jnp.zeros(ref.shape, ref.dtype)

    for r, val in enumerate(rows):
        for c in range(ref.shape[0]):
            ref.at[c][pl.ds(r, tb, stride=SPLIT_ROWS), :] = val[:, c * LANES:(c + 1) * LANES]


def _merge_rows(ref, tb, r):
    return jnp.concatenate([ref.at[c][pl.ds(r, tb, stride=SPLIT_ROWS), :] for c in range(ref.shape[0])], axis=1)


def _load_tile(ref, t8):
    return jnp.concatenate([ref.at[c][pl.ds(t8, SPLIT_ROWS), :] for c in range(ref.shape[0])], axis=1)


def _store_tile(ref, t8, val):
    for c in range(ref.shape[0]):
        ref.at[c][pl.ds(t8, SPLIT_ROWS), :] = val[:, c * LANES:(c + 1) * LANES]


def _tile_scratch(tb, width):
    return pltpu.VMEM((width // LANES, tb * SPLIT_ROWS, LANES), F32)


def _hi_lo(x):
    hi = x.astype(BF16).astype(F32)
    return hi, x - hi


def _peer_u_body(idx_ref, idx_next_ref, h_ref, gate_ref, tab_ref, act_ref, *scratch, tb):
    bufs, (r_ref,) = scratch[:GATHER_BUFS], scratch[GATHER_BUFS:]
    half = D_MODEL // 2
    ones = jnp.ones((SPLIT_ROWS, 2 * LANES), BF16)

    def compute(t, g_ref):
        t8 = pl.multiple_of(t * SPLIT_ROWS, SPLIT_ROWS)
        acc = jnp.zeros((N_ACT, LANES), F32)
        hrow = h_ref[pl.ds(t, 1), :]
        for s in range(ROWS_PER_EXPERT):
            w = g_ref[pl.ds(s, N_ACT, stride=ROWS_PER_EXPERT), :]
            lo = pltpu.bitcast(w << 16, F32)
            hi = pltpu.bitcast(w & jnp.int32(-65536), F32)
            acc = (acc + lo * hrow[:, s * LANES:(s + 1) * LANES]
                   + hi * hrow[:, half + s * LANES:half + (s + 1) * LANES])
            if s == ROWS_PER_EXPERT - 1:
                a_hi, a_lo = _hi_lo(acc)
                terms = jnp.concatenate([a_hi.astype(BF16), a_lo.astype(BF16)], axis=1)
                r_ref[0, pl.ds(t8, SPLIT_ROWS), :] = lax.dot_general(ones, terms, NT_DIMS,
                                                                     preferred_element_type=F32)
            yield

    _token_pipeline(tb, idx_ref, idx_next_ref, tab_ref, compute, bufs, compute_span=PEER_U_SPAN)
    a = _merge_rows(r_ref, tb, 0)
    act_ref[...] = _gelu(a) * jnp.transpose(gate_ref[...])


def _peer_u(idx, h2, gates_t, upk, tb):
    nb = idx.shape[0]
    return pl.pallas_call(
        functools.partial(_peer_u_body, tb=tb),
        grid=(nb,),
        in_specs=_index_specs(nb, tb) + [pl.BlockSpec((tb, D_MODEL), lambda i: (i, 0)),
                                        pl.BlockSpec((None, N_ACT, tb), lambda i: (i, 0, 0)), _whole()],
        out_specs=pl.BlockSpec((None, tb, N_ACT), lambda i: (i, 0, 0)),
        out_shape=jax.ShapeDtypeStruct((nb, tb, N_ACT), F32),
        scratch_shapes=[pltpu.VMEM((N_ACT * ROWS_PER_EXPERT, LANES), jnp.int32)] * GATHER_BUFS
        + [_tile_scratch(tb, N_ACT)],
        compiler_params=_cparams(("arbitrary",)),
        name="peer_u",
    )(idx, idx, h2, gates_t, upk)


def _peer_v_body(idx_ref, idx_next_ref, act_ref, x1_ref, tab_ref, out_ref, *scratch, tb):
    bufs, (l_ref, o_ref) = scratch[:GATHER_BUFS], scratch[GATHER_BUFS:]
    ahi, alo = _hi_lo(act_ref[...])
    ahi, alo = ahi.astype(BF16), alo.astype(BF16)
    j2 = 2 * lax.broadcasted_iota(jnp.int32, (N_ACT, 2 * N_ACT), 0)
    col = lax.broadcasted_iota(jnp.int32, (N_ACT, 2 * N_ACT), 1)
    spread = [jnp.where(col == j2 + b, 1.0, 0.0).astype(BF16) for b in range(2)]
    _split_rows(l_ref, tb, [jnp.dot(a, e, preferred_element_type=F32) for e in spread for a in (ahi, alo)])

    def compute(t, g_ref):
        t8 = pl.multiple_of(t * SPLIT_ROWS, SPLIT_ROWS)
        lhs = _load_tile(l_ref, t8).astype(BF16)
        for s in range(ROWS_PER_EXPERT):
            o_ref.at[s][pl.ds(t8, SPLIT_ROWS), :] = jnp.dot(lhs, _expert_weights(g_ref, s),
                                                             preferred_element_type=F32)
            yield

    _token_pipeline(tb, idx_ref, idx_next_ref, tab_ref, compute, bufs, compute_span=PEER_V_SPAN)
    o = [_merge_rows(o_ref, tb, k) for k in range(4)]
    out_ref[...] = x1_ref[...] + jnp.concatenate([o[0] + o[1], o[2] + o[3]], axis=1)


def _peer_v(idx, act, x1, vpk, tb):
    nb = idx.shape[0]
    return pl.pallas_call(
        functools.partial(_peer_v_body, tb=tb),
        grid=(nb,),
        in_specs=_index_specs(nb, tb) + [pl.BlockSpec((None, tb, N_ACT), lambda i: (i, 0, 0)),
                                        pl.BlockSpec((tb, D_MODEL), lambda i: (i, 0)), _whole()],
        out_specs=pl.BlockSpec((tb, D_MODEL), lambda i: (i, 0)),
        out_shape=jax.ShapeDtypeStruct(x1.shape, F32),
        scratch_shapes=[pltpu.VMEM((N_ACT * ROWS_PER_EXPERT, LANES), jnp.int32)] * GATHER_BUFS
        + [_tile_scratch(tb, 2 * N_ACT), _tile_scratch(tb, D_MODEL // 2)],
        compiler_params=_cparams(("arbitrary",)),
        name="peer_v",
    )(idx, idx, act, x1, vpk)


def _tile(n, pref):
    t = min(pref, n)
    while n % t:
        t //= 2
    return t


def kernel(x, mem, norm1_g, w_in, b_gate, b_forget, fox_q_g, fox_k_g, swa_q_g, swa_k_g, swa_sinks, mem_norm_g, w_mem_kv, mem_q_g, mem_k_g, w_fox_o, w_swa_o, w_mem_o, w_out, norm2_g, w_peer_q, peer_keys1, peer_keys2, peer_u, peer_v):
    b, s, d = x.shape
    n = b * s
    assert d == D_MODEL and s % 256 == 0 and mem.shape[1] == N_MEM
    depth = norm1_g.shape[0]
    xf = x.reshape(n, d)
    memf = mem.reshape(b * N_MEM, d)
    tb = BLOCK
    for l in range(depth):
        w = w_in[l]
        o_ff, o_sq, o_sk, o_sv, o_mq, o_gl = 1536, 1544, 2056, 2184, 2312, 2824
        w_att = jnp.concatenate(
            [w[:, 0:o_ff], w[:, o_sq:o_gl], w[:, o_ff:o_sq], jnp.zeros((d, LANES - FOX_HEADS), F32)],
            axis=1).astype(BF16)
        w_gate = w[:, o_gl:].astype(BF16)
        tile2 = lambda g: jnp.concatenate([g, g])
        gains = jnp.stack([tile2(fox_q_g[l]) * HEAD_D ** -0.5, tile2(fox_k_g[l]),
                           tile2(swa_q_g[l]) * HEAD_D ** -0.5, tile2(swa_k_g[l]),
                           mem_q_g[l] * MEM_HD ** -0.5] + [jnp.zeros((LANES,), F32)] * 3)
        bf = jnp.concatenate([b_forget[l], jnp.zeros((LANES - FOX_HEADS,), F32)]).reshape(1, LANES)
        g1 = norm1_g[l].reshape(1, d)

        fq, fk, fv, sq, sk, sv, mq, lf = _inproj(xf, g1, w_att, gains, bf, _tile(n, 512))
        crow = _cumsum(lf.reshape(b, s, LANES))
        o_fox = _fox(fq, fk, fv, crow.reshape(b * FOX_HEADS, 1, s), b, s, 512)
        o_swa = _swa(swa_sinks[l], sq, sk, sv, b, s)
        mk, mv = _memkv(memf, mem_norm_g[l].reshape(1, d), w_mem_kv[l].astype(BF16), mem_k_g[l].reshape(1, LANES))
        o_mem = _mem(mq, mk, mv, b, s, 256)

        x1, h2, qp = _merge(xf, o_fox, o_swa, o_mem, g1, w_gate, b_gate[l].reshape(1, 3 * d),
                            w_fox_o[l].astype(BF16), w_swa_o[l].astype(BF16), w_mem_o[l].astype(BF16),
                            w_out[l].astype(BF16), norm2_g[l].reshape(1, d), w_peer_q[l].astype(BF16), _tile(n, 256))

        e_t, gates_t = _topk(qp, peer_keys1[l].astype(BF16), peer_keys2[l].astype(BF16), tb)
        idx = _group_index_layout(jnp.swapaxes(e_t, 1, 2) * ROWS_PER_EXPERT, GATHER_BUFS // 2)
        act_t = _peer_u(idx, h2, gates_t, _pack_table(peer_u[l]), tb)
        xf = _peer_v(idx, act_t, x1, _pack_table(peer_v[l]), tb)
    return xf.reshape(b, s, d)
```
